```python
import math
import jax, jax.numpy as jnp
from jax import lax
import numpy as np

D_MODEL = 2048
BATCH = 16
SEQ = 256
DEPTH = 4
DEC_BATCH = 4
DEC_SEQ = 4096
PAST_LEN = 512

GRID_W = 64
Q_BLOCK = 128
N_MIXERS = 3
N_A = (DEPTH + 2) // 3
N_B = (DEPTH + 1) // 3
N_C = DEPTH // 3
ROPE_BASE = 10000.0
EPS = 1e-6
H_A = 16
Q_LORA = 512
KV_LORA = 256
NOPE_A = 128
ROPE_A = 64
V_A = 128
H_B = 16
DH_B = 64
D_B = H_B * 2 * DH_B
H_C = 16
KVH_C = 4
HD_C = 128
G_C = H_C // KVH_C
D_FF = 5632
CONV_W = 3

kernel_name = "hybrid_mla_diff_gqa_convffn_prefix_dit"


def rms_norm(x, g):
    xf = x.astype(jnp.float32)
    y = xf * lax.rsqrt(jnp.mean(xf * xf, axis=-1, keepdims=True) + EPS)
    return (y * g.astype(jnp.float32)).astype(x.dtype)


def modulate(x, shift, scale):
    return x * (1 + scale) + shift


def rope_1d(x, pos):
    d = x.shape[-1]
    half = d // 2
    freqs = ROPE_BASE ** (-jnp.arange(half, dtype=jnp.float32) / half)
    ang = pos[:, None] * freqs[None, :]
    shape = (1, x.shape[1]) + (1,) * (x.ndim - 3) + (half,)
    cos = jnp.cos(ang).reshape(shape)
    sin = jnp.sin(ang).reshape(shape)
    x1 = x[..., :half].astype(jnp.float32)
    x2 = x[..., half:].astype(jnp.float32)
    return jnp.concatenate([x1 * cos - x2 * sin, x1 * sin + x2 * cos], axis=-1).astype(x.dtype)


def rope_2d(x, row, col):
    r = x.shape[-1] // 2
    return jnp.concatenate([rope_1d(x[..., :r], row), rope_1d(x[..., r:], col)], axis=-1)


def sweep_query_blocks(fn, *qs):
    b, s = qs[0].shape[:2]
    nb = s // Q_BLOCK
    blocks = tuple(jnp.moveaxis(q.reshape((b, nb, Q_BLOCK) + q.shape[2:]), 1, 0) for q in qs)
    out = lax.map(lambda blk: fn(*blk), blocks)
    out = jnp.moveaxis(out, 0, 1)
    return out.reshape((b, s) + out.shape[3:])


def mla_project(h, w_down, q_norm, kv_norm, w_uq, row, col):
    b, s, _ = h.shape
    down = h @ w_down
    cq = down[..., :Q_LORA]
    ckv = rms_norm(down[..., Q_LORA:Q_LORA + KV_LORA], kv_norm)
    kpe = down[..., Q_LORA + KV_LORA:]
    q = (rms_norm(cq, q_norm) @ w_uq).reshape(b, s, H_A, NOPE_A + ROPE_A)
    q_nope, q_pe = q[..., :NOPE_A], q[..., NOPE_A:]
    if row is not None:
        q_pe = rope_2d(q_pe, row, col)
        kpe = rope_2d(kpe, row, col)
    return q_nope, q_pe, ckv, kpe


def mla_attend(q_nope, q_pe, ckv, kpe, w_ukv, w_o):
    b, s = q_nope.shape[:2]
    l = ckv.shape[1]
    kv = (ckv @ w_ukv).reshape(b, l, H_A, NOPE_A + V_A)
    k_nope, v = kv[..., :NOPE_A], kv[..., NOPE_A:]
    scale = 1.0 / math.sqrt(NOPE_A + ROPE_A)

    def block(qn, qp):
        sc = (jnp.einsum('bqhd,bkhd->bhqk', qn, k_nope) + jnp.einsum('bqhr,bkr->bhqk', qp, kpe)).astype(jnp.float32) * scale
        p = jax.nn.softmax(sc, axis=-1)
        return jnp.einsum('bhqk,bkhd->bqhd', p.astype(v.dtype), v)

    o = sweep_query_blocks(block, q_nope, q_pe)
    return o.reshape(b, s, H_A * V_A) @ w_o


def mla_mixer(hp, hs, cache_ckv, cache_kpe, row, col, w_down, q_norm, kv_norm, w_uq, w_ukv, w_o):
    qn_p, qpe_p, ckv_p, kpe_p = mla_project(hp, w_down, q_norm, kv_norm, w_uq, None, None)
    out_p = mla_attend(qn_p, qpe_p, ckv_p, kpe_p, w_ukv, w_o)
    qn_s, qpe_s, ckv_s, kpe_s = mla_project(hs, w_down, q_norm, kv_norm, w_uq, row, col)
    out_s = mla_attend(qn_s, qpe_s, jnp.concatenate([cache_ckv, ckv_s], axis=1),
                       jnp.concatenate([cache_kpe, kpe_s], axis=1), w_ukv, w_o)
    return out_p, out_s, ckv_p, kpe_p


def diff_project(h, w_qkv, row, col):
    b, s, _ = h.shape
    qkv = h @ w_qkv
    q = qkv[..., :D_B].reshape(b, s, H_B, 2, DH_B)
    k = qkv[..., D_B:2 * D_B].reshape(b, s, H_B, 2, DH_B)
    v = qkv[..., 2 * D_B:].reshape(b, s, H_B, 2 * DH_B)
    if row is not None:
        q = rope_2d(q, row, col)
        k = rope_2d(k, row, col)
    return q, k, v


def diff_attend(q, k, v, lam, lam_init, subln, w_o):
    b, s = q.shape[:2]
    scale = 1.0 / math.sqrt(DH_B)

    def block(qb):
        sc = jnp.einsum('bqhcd,bkhcd->cbhqk', qb, k).astype(jnp.float32) * scale
        p = jax.nn.softmax(sc, axis=-1)
        a = p[0] - lam * p[1]
        return jnp.einsum('bhqk,bkhe->bqhe', a.astype(v.dtype), v)

    o = sweep_query_blocks(block, q)
    o = rms_norm(o, subln) * (1.0 - lam_init)
    return o.reshape(b, s, D_B) @ w_o


def diff_mixer(hp, hs, cache_k, cache_v, row, col, w_qkv, lam_vecs, subln, w_o, lam_init):
    lv = lam_vecs.astype(jnp.float32)
    lam = jnp.exp(jnp.sum(lv[0] * lv[1])) - jnp.exp(jnp.sum(lv[2] * lv[3])) + lam_init
    q_p, k_p, v_p = diff_project(hp, w_qkv, None, None)
    out_p = diff_attend(q_p, k_p, v_p, lam, lam_init, subln, w_o)
    q_s, k_s, v_s = diff_project(hs, w_qkv, row, col)
    out_s = diff_attend(q_s, jnp.concatenate([cache_k, k_s], axis=1),
                        jnp.concatenate([cache_v, v_s], axis=1), lam, lam_init, subln, w_o)
    return out_p, out_s, k_p, v_p


def gqa_project(h, w_qkv, q_norm, k_norm, row, col):
    b, s, _ = h.shape
    qkv = h @ w_qkv
    q = rms_norm(qkv[..., :H_C * HD_C].reshape(b, s, H_C, HD_C), q_norm)
    k = rms_norm(qkv[..., H_C * HD_C:(H_C + KVH_C) * HD_C].reshape(b, s, KVH_C, HD_C), k_norm)
    v = qkv[..., (H_C + KVH_C) * HD_C:].reshape(b, s, KVH_C, HD_C)
    if row is not None:
        q = rope_2d(q, row, col)
        k = rope_2d(k, row, col)
    return q, k, v


def gqa_attend(q, k, v, w_o):
    b, s = q.shape[:2]
    qg = q.reshape(b, s, KVH_C, G_C, HD_C)
    scale = 1.0 / math.sqrt(HD_C)

    def block(qb):
        sc = jnp.einsum('bqkgd,blkd->bkgql', qb, k).astype(jnp.float32) * scale
        p = jax.nn.softmax(sc, axis=-1)
        return jnp.einsum('bkgql,blkd->bqkgd', p.astype(v.dtype), v)

    o = sweep_query_blocks(block, qg)
    return o.reshape(b, s, H_C * HD_C) @ w_o


def gqa_mixer(hp, hs, cache_k, cache_v, row, col, w_qkv, q_norm, k_norm, w_o):
    q_p, k_p, v_p = gqa_project(hp, w_qkv, q_norm, k_norm, None, None)
    out_p = gqa_attend(q_p, k_p, v_p, w_o)
    q_s, k_s, v_s = gqa_project(hs, w_qkv, q_norm, k_norm, row, col)
    out_s = gqa_attend(q_s, jnp.concatenate([cache_k, k_s], axis=1),
                       jnp.concatenate([cache_v, v_s], axis=1), w_o)
    return out_p, out_s, k_p, v_p


def conv_ffn(h, w_in, conv_w, conv_b, w_down):
    u = h @ w_in
    up = jnp.pad(u, ((0, 0), (1, 1), (0, 0)))
    u = up[:, :-2] * conv_w[0] + up[:, 1:-1] * conv_w[1] + up[:, 2:] * conv_w[2] + conv_b
    gate, val = u[..., :D_FF], u[..., D_FF:]
    return (jax.nn.silu(gate) * val) @ w_down


def lambda_init_for(layer):
    return 0.8 - 0.6 * math.exp(-0.3 * layer)


def setup_inputs(seed: int = 0) -> dict:
    key = jax.random.key(seed)
    ks = iter(jax.random.split(key, 48))

    def nrm(shape, scale=1.0):
        return jax.random.normal(next(ks), shape, jnp.float32) * scale

    def gain(shape):
        return 1.0 + nrm(shape, 0.1)

    D = D_MODEL
    return {
        "x_prompt": nrm((BATCH, SEQ, D)),
        "x_sample": nrm((DEC_BATCH, DEC_SEQ, D)),
        "c": nrm((DEC_BATCH, D)),
        "cache_a_ckv": nrm((DEC_BATCH, N_A, PAST_LEN, KV_LORA)),
        "cache_a_kpe": nrm((DEC_BATCH, N_A, PAST_LEN, ROPE_A)),
        "cache_b_k": nrm((DEC_BATCH, N_B, PAST_LEN, H_B, 2, DH_B)),
        "cache_b_v": nrm((DEC_BATCH, N_B, PAST_LEN, H_B, 2 * DH_B)),
        "cache_c_k": nrm((DEC_BATCH, N_C, PAST_LEN, KVH_C, HD_C)),
        "cache_c_v": nrm((DEC_BATCH, N_C, PAST_LEN, KVH_C, HD_C)),
        "c_ctx": nrm((D,)),
        "norm_g": gain((DEPTH, 4, D)),
        "w_mod": nrm((DEPTH, D, 6 * D), 0.5 * D ** -0.5),
        "b_mod": nrm((DEPTH, 6 * D), 0.01),
        "ffn_w_in": nrm((DEPTH, D, 2 * D_FF), D ** -0.5),
        "ffn_conv_w": nrm((DEPTH, CONV_W, 2 * D_FF), CONV_W ** -0.5),
        "ffn_conv_b": nrm((DEPTH, 2 * D_FF), 0.01),
        "ffn_w_down": nrm((DEPTH, D_FF, D), D_FF ** -0.5),
        "a_w_down": nrm((N_A, D, Q_LORA + KV_LORA + ROPE_A), D ** -0.5),
        "a_q_norm": gain((N_A, Q_LORA)),
        "a_kv_norm": gain((N_A, KV_LORA)),
        "a_w_uq": nrm((N_A, Q_LORA, H_A * (NOPE_A + ROPE_A)), Q_LORA ** -0.5),
        "a_w_ukv": nrm((N_A, KV_LORA, H_A * (NOPE_A + V_A)), KV_LORA ** -0.5),
        "a_w_o": nrm((N_A, H_A * V_A, D), (H_A * V_A) ** -0.5),
        "b_w_qkv": nrm((N_B, D, 3 * D_B), D ** -0.5),
        "b_lambda": nrm((N_B, 4, DH_B), 0.1),
        "b_subln": gain((N_B, 2 * DH_B)),
        "b_w_o": nrm((N_B, D_B, D), D_B ** -0.5),
        "c_w_qkv": nrm((N_C, D, (H_C + 2 * KVH_C) * HD_C), D ** -0.5),
        "c_q_norm": gain((N_C, HD_C)),
        "c_k_norm": gain((N_C, HD_C)),
        "c_w_o": nrm((N_C, H_C * HD_C, D), (H_C * HD_C) ** -0.5),
    }


def reference(x_prompt, x_sample, c, cache_a_ckv, cache_a_kpe, cache_b_k, cache_b_v, cache_c_k, cache_c_v,
              c_ctx, norm_g, w_mod, b_mod, ffn_w_in, ffn_conv_w, ffn_conv_b, ffn_w_down,
              a_w_down, a_q_norm, a_kv_norm, a_w_uq, a_w_ukv, a_w_o,
              b_w_qkv, b_lambda, b_subln, b_w_o,
              c_w_qkv, c_q_norm, c_k_norm, c_w_o):
    t = x_sample.shape[1]
    rows = t // GRID_W
    row = jnp.repeat(jnp.arange(rows, dtype=jnp.float32), GRID_W)
    col = jnp.tile(jnp.arange(GRID_W, dtype=jnp.float32), rows)

    xp, xs = x_prompt, x_sample
    silu_ctx = jax.nn.silu(c_ctx)
    silu_c = jax.nn.silu(c)
    st_a_ckv, st_a_kpe, st_b_k, st_b_v, st_c_k, st_c_v = [], [], [], [], [], []

    for l in range(DEPTH):
        kind, j = l % N_MIXERS, l // N_MIXERS
        mp = jnp.split(silu_ctx @ w_mod[l] + b_mod[l], 6, axis=-1)
        ms = [m[:, None, :] for m in jnp.split(silu_c @ w_mod[l] + b_mod[l], 6, axis=-1)]

        hp = modulate(rms_norm(xp, norm_g[l, 0]), mp[0], mp[1])
        hs = modulate(rms_norm(xs, norm_g[l, 0]), ms[0], ms[1])
        if kind == 0:
            op, osm, s0, s1 = mla_mixer(hp, hs, cache_a_ckv[:, j], cache_a_kpe[:, j], row, col,
                                        a_w_down[j], a_q_norm[j], a_kv_norm[j], a_w_uq[j], a_w_ukv[j], a_w_o[j])
            st_a_ckv.append(s0)
            st_a_kpe.append(s1)
        elif kind == 1:
            op, osm, s0, s1 = diff_mixer(hp, hs, cache_b_k[:, j], cache_b_v[:, j], row, col,
                                         b_w_qkv[j], b_lambda[j], b_subln[j], b_w_o[j], lambda_init_for(l))
            st_b_k.append(s0)
            st_b_v.append(s1)
        else:
            op, osm, s0, s1 = gqa_mixer(hp, hs, cache_c_k[:, j], cache_c_v[:, j], row, col,
                                        c_w_qkv[j], c_q_norm[j], c_k_norm[j], c_w_o[j])
            st_c_k.append(s0)
            st_c_v.append(s1)
        xp = xp + mp[2] * rms_norm(op, norm_g[l, 1])
        xs = xs + ms[2] * rms_norm(osm, norm_g[l, 1])

        hp = modulate(rms_norm(xp, norm_g[l, 2]), mp[3], mp[4])
        hs = modulate(rms_norm(xs, norm_g[l, 2]), ms[3], ms[4])
        fp = conv_ffn(hp, ffn_w_in[l], ffn_conv_w[l], ffn_conv_b[l], ffn_w_down[l])
        fs = conv_ffn(hs, ffn_w_in[l], ffn_conv_w[l], ffn_conv_b[l], ffn_w_down[l])
        xp = xp + mp[5] * rms_norm(fp, norm_g[l, 3])
        xs = xs + ms[5] * rms_norm(fs, norm_g[l, 3])

    state_a_ckv = jnp.stack(st_a_ckv, axis=1)
    state_a_kpe = jnp.stack(st_a_kpe, axis=1)
    state_b_k = jnp.stack(st_b_k, axis=1)
    state_b_v = jnp.stack(st_b_v, axis=1)
    state_c_k = jnp.stack(st_c_k, axis=1)
    state_c_v = jnp.stack(st_c_v, axis=1)
    return (xp, xs, state_a_ckv, state_a_kpe, state_b_k, state_b_v, state_c_k, state_c_v)
```

```python
import functools
import math

import jax
import jax.numpy as jnp
import numpy as np
from jax import lax
from jax.experimental import pallas as pl
from jax.experimental.pallas import tpu as pltpu

F32 = jnp.float32
BF16 = jnp.bfloat16

EPS = 1e-6
ROPE_BASE = 10000.0
GRID_W = 64
N_MIXERS = 3
H_A = 16
NOPE_A = 128
ROPE_A = 64
V_A = 128
H_B = 16
DH_B = 64
H_C = 16
KVH_C = 4
HD_C = 128

LANES = 128
VMEM_LIMIT = 56 * 1024 * 1024
ROW_TILE = 512
ATTN_ROWS = 512
KV_CHUNK = 512


def _params(*sem):
    return pltpu.CompilerParams(dimension_semantics=sem, vmem_limit_bytes=VMEM_LIMIT)


def _largest_tile(n, cap):
    for tile in range(cap, 0, -LANES):
        if n % tile == 0:
            return tile
    return n


def _dot(a, b):
    return jnp.dot(a, b, preferred_element_type=F32)


def _dot_nt(a, b):
    return lax.dot_general(a, b, (((1,), (1,)), ((), ())), preferred_element_type=F32)


def _rms(x, g):
    return x * lax.rsqrt(jnp.mean(x * x, axis=-1, keepdims=True) + EPS) * g


def _modnorm(x, g, shift, scale):
    return _rms(x, g) * (1.0 + scale) + shift


def _rope(x, cos, sin, half):
    w = x.shape[-1]
    lane = lax.broadcasted_iota(jnp.int32, x.shape, 1)
    first = (lane & (2 * half - 1)) < half
    partner = jnp.where(first, pltpu.roll(x, w - half, 1), pltpu.roll(x, half, 1))
    return x * cos + partner * sin


def _rope_base(t, dim):
    pos = jnp.arange(t)
    row = (pos // GRID_W).astype(F32)
    col = (pos % GRID_W).astype(F32)
    r = dim // 2
    half = r // 2
    freqs = ROPE_BASE ** (-jnp.arange(half, dtype=F32) / half)
    lane = np.arange(dim)
    idx = (lane % r) % half
    first = (lane % r) < half
    use_col = (lane // r) == 1
    ang = jnp.where(use_col[None, :], col[:, None], row[:, None]) * freqs[idx][None, :]
    return jnp.cos(ang), jnp.where(first[None, :], -jnp.sin(ang), jnp.sin(ang))


def _rope_tables(t, dim, left, right, reps=1, scale=1.0):
    cos, sin = _rope_base(t, dim)
    cos = jnp.concatenate([jnp.ones((t, left), F32)] + [cos] * reps + [jnp.ones((t, right), F32)], axis=1)
    sin = jnp.concatenate([jnp.zeros((t, left), F32)] + [sin] * reps + [jnp.zeros((t, right), F32)], axis=1)
    return cos * scale, sin * scale


def _mod_kernel(c_ref, w_ref, b_ref, o_ref):
    c = c_ref[...]
    s = c * (1.0 / (1.0 + jnp.exp(-c)))
    o_ref[0] = _dot(s.astype(BF16), w_ref[0].astype(BF16)) + b_ref[0]


def _modulation(cvec, w_mod, b_mod):
    depth, d, n = w_mod.shape
    tn = _largest_tile(n, 1024)
    return pl.pallas_call(
        _mod_kernel,
        out_shape=jax.ShapeDtypeStruct((depth, cvec.shape[0], n), F32),
        grid=(depth, n // tn),
        in_specs=[pl.BlockSpec(cvec.shape, lambda l, j: (0, 0)),
                  pl.BlockSpec((1, d, tn), lambda l, j: (l, 0, j)),
                  pl.BlockSpec((1, 1, tn), lambda l, j: (l, 0, j))],
        out_specs=pl.BlockSpec((1, cvec.shape[0], tn), lambda l, j: (l, 0, j)),
        compiler_params=_params("arbitrary", "arbitrary"),
        name="adaln_modulation",
    )(cvec, w_mod, b_mod.reshape(depth, 1, n))


def _tile_rows(t):
    return min(ROW_TILE, t)


def _mla_down_kernel(*refs, ql, kvl, rope):
    if rope:
        x_ref, mod_ref, g_ref, w_ref, qn_ref, kvn_ref, cos_ref, sin_ref, cq_ref, ckv_ref, kpe_ref = refs
    else:
        x_ref, mod_ref, g_ref, w_ref, qn_ref, kvn_ref, cq_ref, ckv_ref, kpe_ref = refs
    h = _modnorm(x_ref[...], g_ref[...], mod_ref[0, 0:1, :], mod_ref[0, 1:2, :])
    d = _dot(h.astype(BF16), w_ref[...])
    cq_ref[...] = _rms(d[:, :ql], qn_ref[...]).astype(BF16)
    ckv_ref[...] = _rms(d[:, ql:ql + kvl], kvn_ref[...])
    kpe = d[:, ql + kvl:]
    if rope:
        kpe = _rope(kpe, cos_ref[...], sin_ref[...], ROPE_A // 4)
    kpe_ref[...] = kpe


def _mla_down(x, mod, g, w, qn, kvn, tables, groups):
    m, d = x.shape
    t = m // groups
    tm = _tile_rows(t)
    nt = t // tm
    ql, kvl = qn.shape[1], kvn.shape[1]
    n = w.shape[1]
    rope = tables is not None
    row = lambda gi, i: (gi * nt + i, 0)
    const = lambda gi, i: (0, 0)
    in_specs = [pl.BlockSpec((tm, d), row),
                pl.BlockSpec((1, 6, d), lambda gi, i: (gi, 0, 0)),
                pl.BlockSpec((1, d), const),
                pl.BlockSpec((d, n), const),
                pl.BlockSpec((1, ql), const),
                pl.BlockSpec((1, kvl), const)]
    args = [x, mod, g, w, qn, kvn]
    if rope:
        in_specs += [pl.BlockSpec((tm, LANES), lambda gi, i: (i, 0))] * 2
        args += list(tables)
    return pl.pallas_call(
        functools.partial(_mla_down_kernel, ql=ql, kvl=kvl, rope=rope),
        out_shape=(jax.ShapeDtypeStruct((m, ql), BF16),
                   jax.ShapeDtypeStruct((m, kvl), F32),
                   jax.ShapeDtypeStruct((m, n - ql - kvl), F32)),
        grid=(groups, nt),
        in_specs=in_specs,
        out_specs=(pl.BlockSpec((tm, ql), row), pl.BlockSpec((tm, kvl), row),
                   pl.BlockSpec((tm, n - ql - kvl), row)),
        compiler_params=_params("arbitrary", "arbitrary"),
        name="mla_down_proj",
    )(*args)


def _qkv_kernel(*refs, rope, want_f32, n_q, n_k, pattern, half, qk_norm, q_scale):
    refs = list(refs)
    x_ref, mod_ref, g_ref, w_ref = refs[:4]
    pos = 4
    if qk_norm:
        qn_ref, kn_ref = refs[pos:pos + 2]
        pos += 2
    if rope:
        cos_ref, sin_ref = refs[pos:pos + 2]
        pos += 2
    ob_ref = refs[pos]
    pos += 1
    if want_f32:
        of_ref = refs[pos]
        pos += 1
    h_ref = refs[pos]
    j = pl.program_id(2)

    @pl.when(j == 0)
    def _():
        h_ref[...] = _modnorm(x_ref[...], g_ref[...], mod_ref[0, 0:1, :], mod_ref[0, 1:2, :]).astype(BF16)

    acc = _dot(h_ref[...], w_ref[...])
    tn = acc.shape[1]

    def emit(kind):
        for c in range(tn // pattern):
            sl = slice(c * pattern, (c + 1) * pattern)
            y = acc[:, sl]
            if kind != "v":
                if qk_norm:
                    y = _rms(y, qn_ref[...] if kind == "q" else kn_ref[...])
                if want_f32:
                    of_ref[:, sl] = y
                if rope:
                    y = _rope(y, cos_ref[...], sin_ref[...], half)
                if kind == "q":
                    y = y * q_scale
            elif want_f32:
                of_ref[:, sl] = y
            ob_ref[:, sl] = y.astype(BF16)

    pl.when(j < n_q)(lambda: emit("q"))
    pl.when((j >= n_q) & (j < n_q + n_k))(lambda: emit("k"))
    pl.when(j >= n_q + n_k)(lambda: emit("v"))


def _qkv_proj(x, mod, g, w, groups, *, tn, n_q, n_k, pattern, half, q_scale, tables=None, norms=None,
              want_f32=False):
    m, d = x.shape
    n = w.shape[1]
    t = m // groups
    tm = _tile_rows(t)
    nt = t // tm
    rope = tables is not None
    qk_norm = norms is not None
    row = lambda gi, i, j: (gi * nt + i, 0)
    const = lambda gi, i, j: (0, 0)
    tile = lambda gi, i, j: (gi * nt + i, j)
    in_specs = [pl.BlockSpec((tm, d), row),
                pl.BlockSpec((1, 6, d), lambda gi, i, j: (gi, 0, 0)),
                pl.BlockSpec((1, d), const),
                pl.BlockSpec((d, tn), lambda gi, i, j: (0, j))]
    args = [x, mod, g, w]
    if qk_norm:
        in_specs += [pl.BlockSpec((1, pattern), const)] * 2
        args += list(norms)
    if rope:
        in_specs += [pl.BlockSpec((tm, pattern), lambda gi, i, j: (i, 0))] * 2
        args += list(tables)
    out_shape = [jax.ShapeDtypeStruct((m, n), BF16)]
    out_specs = [pl.BlockSpec((tm, tn), tile)]
    if want_f32:
        out_shape.append(jax.ShapeDtypeStruct((m, n), F32))
        out_specs.append(pl.BlockSpec((tm, tn), tile))
    return pl.pallas_call(
        functools.partial(_qkv_kernel, rope=rope, want_f32=want_f32, n_q=n_q, n_k=n_k, pattern=pattern,
                          half=half, qk_norm=qk_norm, q_scale=q_scale),
        out_shape=tuple(out_shape),
        grid=(groups, nt, n // tn),
        in_specs=in_specs,
        out_specs=tuple(out_specs),
        scratch_shapes=[pltpu.VMEM((tm, d), BF16)],
        compiler_params=_params("arbitrary", "arbitrary", "arbitrary"),
        name="qkv_proj",
    )(*args)


def _mla_uq_kernel(*refs, rope, pattern, scale):
    if rope:
        x_ref, w_ref, cos_ref, sin_ref, o_ref = refs
    else:
        x_ref, w_ref, o_ref = refs
    acc = _dot(x_ref[...], w_ref[...])
    for c in range(acc.shape[1] // pattern):
        sl = slice(c * pattern, (c + 1) * pattern)
        y = acc[:, sl]
        if rope:
            y = _rope(y, cos_ref[...], sin_ref[...], ROPE_A // 4)
        o_ref[:, sl] = (y * scale).astype(BF16)


def _mla_uq(cq, w, tables, t, scale):
    m, k = cq.shape
    n = w.shape[1]
    tm = _tile_rows(t)
    nt = t // tm
    tn = min(1024, n)
    pattern = 2 * LANES
    rope = tables is not None
    in_specs = [pl.BlockSpec((tm, k), lambda i, j: (i, 0)), pl.BlockSpec((k, tn), lambda i, j: (0, j))]
    args = [cq, w]
    if rope:
        in_specs += [pl.BlockSpec((tm, pattern), lambda i, j: (i % nt, 0))] * 2
        args += list(tables)
    return pl.pallas_call(
        functools.partial(_mla_uq_kernel, rope=rope, pattern=pattern, scale=scale),
        out_shape=jax.ShapeDtypeStruct((m, n), BF16),
        grid=(m // tm, n // tn),
        in_specs=in_specs,
        out_specs=pl.BlockSpec((tm, tn), lambda i, j: (i, j)),
        compiler_params=_params("arbitrary", "arbitrary"),
        name="mla_q_up_proj",
    )(*args)


def _mla_ukv_kernel(x_ref, w_ref, kpe_ref, k_ref, v_ref, *, heads):
    acc = _dot(x_ref[...], w_ref[...])
    kpe = kpe_ref[...]
    for h in range(heads):
        k_ref[:, 2 * h * LANES:(2 * h + 1) * LANES] = acc[:, 2 * h * LANES:(2 * h + 1) * LANES].astype(BF16)
        k_ref[:, (2 * h + 1) * LANES:(2 * h + 2) * LANES] = kpe
        v_ref[:, h * LANES:(h + 1) * LANES] = acc[:, (2 * h + 1) * LANES:(2 * h + 2) * LANES].astype(BF16)


def _mla_ukv(ckv, w, kpe):
    m, k = ckv.shape
    n = w.shape[1]
    tm = 512 if m % 512 == 0 else m
    heads = min(4, H_A)
    tn = heads * 2 * LANES
    return pl.pallas_call(
        functools.partial(_mla_ukv_kernel, heads=heads),
        out_shape=(jax.ShapeDtypeStruct((m, n), BF16), jax.ShapeDtypeStruct((m, n // 2), BF16)),
        grid=(m // tm, n // tn),
        in_specs=[pl.BlockSpec((tm, k), lambda i, j: (i, 0)),
                  pl.BlockSpec((k, tn), lambda i, j: (0, j)),
                  pl.BlockSpec((tm, LANES), lambda i, j: (i, 0))],
        out_specs=(pl.BlockSpec((tm, tn), lambda i, j: (i, j)),
                   pl.BlockSpec((tm, tn // 2), lambda i, j: (i, j))),
        compiler_params=_params("arbitrary", "arbitrary"),
        name="mla_kv_up_proj",
    )(ckv, w, kpe)


def _attn_kernel(*refs, diff, reps, dq, dv, tq, length, tk, lam_init):
    if diff:
        q_ref, k_ref, v_ref, lam_ref, subln_ref, o_ref, s_ref, p_ref = refs
    else:
        q_ref, k_ref, v_ref, o_ref, s_ref, p_ref = refs
    q = q_ref[0]
    if diff:
        lane = lax.broadcasted_iota(jnp.int32, q.shape, 1)
        zero = jnp.zeros_like(q)
        qs = jnp.concatenate([jnp.where(lane < dq // 2, q, zero), jnp.where(lane >= dq // 2, q, zero)], axis=0)
    elif reps > 1:
        qs = jnp.concatenate([q[:, r * dq:(r + 1) * dq] for r in range(reps)], axis=0)
    else:
        qs = q
    rows = qs.shape[0]
    nc = length // tk
    m = jnp.full((rows, 1), -jnp.inf, F32)
    for c in range(nc):
        s = _dot_nt(qs, k_ref[0, c * tk:(c + 1) * tk, :])
        s_ref[:, c * tk:(c + 1) * tk] = s
        m = jnp.maximum(m, jnp.max(s, axis=-1, keepdims=True))
    l = jnp.zeros((rows, 1), F32)
    for c in range(nc):
        p = jnp.exp(s_ref[:, c * tk:(c + 1) * tk] - m)
        l = l + jnp.sum(p, axis=-1, keepdims=True)
        if diff:
            s_ref[:, c * tk:(c + 1) * tk] = p
        else:
            p_ref[:, c * tk:(c + 1) * tk] = p.astype(BF16)
    if diff:
        lv = lam_ref[...]
        lam = (jnp.exp(jnp.sum(lv[0:1] * lv[1:2], axis=-1, keepdims=True))
               - jnp.exp(jnp.sum(lv[2:3] * lv[3:4], axis=-1, keepdims=True)) + lam_init)
        inv0 = 1.0 / l[:tq]
        inv1 = lam / l[tq:]
        for c in range(nc):
            a = s_ref[0:tq, c * tk:(c + 1) * tk] * inv0 - s_ref[tq:2 * tq, c * tk:(c + 1) * tk] * inv1
            p_ref[:, c * tk:(c + 1) * tk] = a.astype(BF16)
        o = _dot(p_ref[...], v_ref[0])
        o_ref[0] = (_rms(o, subln_ref[...]) * (1.0 - lam_init)).astype(BF16)
    else:
        o = _dot(p_ref[...], v_ref[0]) / l
        for r in range(reps):
            o_ref[0, :, r * dv:(r + 1) * dv] = o[r * tq:(r + 1) * tq].astype(BF16)


def _attention(q, k, v, *, kv_heads, reps, dq, dv, diff=False, lam=None, subln=None, lam_init=0.0,
               q_off=0, k_off=0, v_off=0):
    b, s, _ = q.shape
    length = k.shape[1]
    stack = 2 if diff else reps
    tq = min(ATTN_ROWS // stack, s)
    tk = min(KV_CHUNK, length)
    rows = stack * tq
    in_specs = [pl.BlockSpec((1, tq, reps * dq), lambda bi, h, i: (bi, i, q_off + h)),
                pl.BlockSpec((1, length, dq), lambda bi, h, i: (bi, 0, k_off + h)),
                pl.BlockSpec((1, length, dv), lambda bi, h, i: (bi, 0, v_off + h))]
    args = [q, k, v]
    if diff:
        in_specs += [pl.BlockSpec(lam.shape, lambda bi, h, i: (0, 0)),
                     pl.BlockSpec(subln.shape, lambda bi, h, i: (0, 0))]
        args += [lam, subln]
    return pl.pallas_call(
        functools.partial(_attn_kernel, diff=diff, reps=reps, dq=dq, dv=dv, tq=tq, length=length, tk=tk,
                          lam_init=lam_init),
        out_shape=jax.ShapeDtypeStruct((b, s, kv_heads * reps * dv), BF16),
        grid=(b, kv_heads, s // tq),
        in_specs=in_specs,
        out_specs=pl.BlockSpec((1, tq, reps * dv), lambda bi, h, i: (bi, i, h)),
        scratch_shapes=[pltpu.VMEM((rows, length), F32),
                        pltpu.VMEM((tq if diff else rows, length), BF16)],
        compiler_params=_params("arbitrary", "arbitrary", "arbitrary"),
        name="diff_attention" if diff else "attention",
    )(*args)


def _out_proj_kernel(o_ref, w_ref, x_ref, mod_ref, g_ref, y_ref, *, gate_row):
    f = _dot(o_ref[...], w_ref[...])
    y_ref[...] = x_ref[...] + mod_ref[0, gate_row:gate_row + 1, :] * _rms(f, g_ref[...])


def _out_proj(o, w, x, mod, g, groups, gate_row):
    m, d = x.shape
    k = o.shape[1]
    t = m // groups
    tm = _tile_rows(t)
    nt = t // tm
    row = lambda gi, i: (gi * nt + i, 0)
    const = lambda gi, i: (0, 0)
    return pl.pallas_call(
        functools.partial(_out_proj_kernel, gate_row=gate_row),
        out_shape=jax.ShapeDtypeStruct((m, d), F32),
        grid=(groups, nt),
        in_specs=[pl.BlockSpec((tm, k), row),
                  pl.BlockSpec((k, d), const),
                  pl.BlockSpec((tm, d), row),
                  pl.BlockSpec((1, 6, d), lambda gi, i: (gi, 0, 0)),
                  pl.BlockSpec((1, d), const)],
        out_specs=pl.BlockSpec((tm, d), row),
        compiler_params=_params("arbitrary", "arbitrary"),
        name="out_proj_residual",
    )(o, w, x, mod, g)


HALO = 8


def _ffn_kernel(x_ref, xp_ref, xn_ref, mod_ref, g_ref, go_ref, wg_ref, wv_ref, cwg_ref, cwv_ref, cbg_ref,
                cbv_ref, wd_ref, y_ref, h_ref, acc_ref, *, tm, seq):
    i = pl.program_id(1)
    j = pl.program_id(2)
    nj = pl.num_programs(2)

    @pl.when(j == 0)
    def _():
        shift, scale = mod_ref[0, 3:4, :], mod_ref[0, 4:5, :]
        h_ref[0:tm, :] = _modnorm(x_ref[...], g_ref[...], shift, scale).astype(BF16)
        halo = jnp.concatenate([xp_ref[...], xn_ref[...]], axis=0)
        h_ref[tm:tm + 2 * HALO, :] = _modnorm(halo, g_ref[...], shift, scale).astype(BF16)
        acc_ref[...] = jnp.zeros_like(acc_ref)

    r = lax.broadcasted_iota(jnp.int32, (tm, 1), 0)
    tok = i * tm + r
    seq_first = (tok % seq) == 0
    seq_last = (tok % seq) == seq - 1

    def conv(w_ref, cw_ref, cb_ref):
        u = _dot(h_ref[...], w_ref[...])
        um = u[0:tm]
        prev_row = u[tm + HALO - 1:tm + HALO]
        next_row = u[tm + HALO:tm + HALO + 1]
        up = jnp.where(r == 0, prev_row, pltpu.roll(um, 1, 0))
        un = jnp.where(r == tm - 1, next_row, pltpu.roll(um, tm - 1, 0))
        up = jnp.where(seq_first, 0.0, up)
        un = jnp.where(seq_last, 0.0, un)
        return up * cw_ref[0:1, :] + um * cw_ref[1:2, :] + un * cw_ref[2:3, :] + cb_ref[...]

    gate = conv(wg_ref, cwg_ref, cbg_ref)
    val = conv(wv_ref, cwv_ref, cbv_ref)
    act = gate * (1.0 / (1.0 + jnp.exp(-gate))) * val
    acc_ref[...] += _dot(act.astype(BF16), wd_ref[...])

    @pl.when(j == nj - 1)
    def _():
        y_ref[...] = x_ref[...] + mod_ref[0, 5:6, :] * _rms(acc_ref[...], go_ref[...])


def _conv_ffn(x, mod, g_in, g_out, w_in, conv_w, conv_b, w_down, groups, seq):
    m, d = x.shape
    ff = w_down.shape[0]
    t = m // groups
    tm = _tile_rows(t)
    nt = t // tm
    tf = 512 if ff % 512 == 0 else ff
    nf = ff // tf
    hb = tm // HALO
    last_halo = m // HALO - 1
    row = lambda gi, i, j: (gi * nt + i, 0)
    const = lambda gi, i, j: (0, 0)
    return pl.pallas_call(
        functools.partial(_ffn_kernel, tm=tm, seq=seq),
        out_shape=jax.ShapeDtypeStruct((m, d), F32),
        grid=(groups, nt, nf),
        in_specs=[pl.BlockSpec((tm, d), row),
                  pl.BlockSpec((HALO, d), lambda gi, i, j: (jnp.maximum((gi * nt + i) * hb - 1, 0), 0)),
                  pl.BlockSpec((HALO, d), lambda gi, i, j: (jnp.minimum((gi * nt + i + 1) * hb, last_halo), 0)),
                  pl.BlockSpec((1, 6, d), lambda gi, i, j: (gi, 0, 0)),
                  pl.BlockSpec((1, d), const),
                  pl.BlockSpec((1, d), const),
                  pl.BlockSpec((d, tf), lambda gi, i, j: (0, j)),
                  pl.BlockSpec((d, tf), lambda gi, i, j: (0, nf + j)),
                  pl.BlockSpec((3, tf), lambda gi, i, j: (0, j)),
                  pl.BlockSpec((3, tf), lambda gi, i, j: (0, nf + j)),
                  pl.BlockSpec((1, tf), lambda gi, i, j: (0, j)),
                  pl.BlockSpec((1, tf), lambda gi, i, j: (0, nf + j)),
                  pl.BlockSpec((tf, d), lambda gi, i, j: (j, 0))],
        out_specs=pl.BlockSpec((tm, d), row),
        scratch_shapes=[pltpu.VMEM((tm + 2 * HALO, d), BF16), pltpu.VMEM((tm, d), F32)],
        compiler_params=_params("arbitrary", "arbitrary", "arbitrary"),
        name="conv_ffn",
    )(x, x, x, mod, g_in, g_out, w_in, w_in, conv_w, conv_w, conv_b, conv_b, w_down)


def _pad_lanes(a, width):
    return jnp.pad(a, [(0, 0)] * (a.ndim - 1) + [(0, width - a.shape[-1])])


def _mla_mixer(x, mod, g, groups, t, sample, seq, cache_ckv, cache_kpe, w_down, q_norm, kv_norm, w_uq, w_ukv):
    m = x.shape[0]
    ql, kvl = q_norm.shape[0], kv_norm.shape[0]
    scale = 1.0 / math.sqrt(NOPE_A + ROPE_A)
    kpe_tables = _rope_tables(t, ROPE_A, 0, LANES - ROPE_A) if sample else None
    q_tables = _rope_tables(t, ROPE_A, NOPE_A, 2 * LANES - NOPE_A - ROPE_A) if sample else None
    w_down_p = _pad_lanes(w_down, ql + kvl + LANES).astype(BF16)
    cq, ckv, kpe = _mla_down(x, mod, g, w_down_p, q_norm[None], kv_norm[None], kpe_tables, groups)
    w_uq_p = _pad_lanes(w_uq.reshape(ql, H_A, NOPE_A + ROPE_A), 2 * LANES).reshape(ql, H_A * 2 * LANES)
    q = _mla_uq(cq, w_uq_p.astype(BF16), q_tables, t, scale)
    if sample:
        b = groups
        ckv_all = jnp.concatenate([cache_ckv.astype(BF16), ckv.astype(BF16).reshape(b, t, kvl)], axis=1)
        kpe_all = jnp.concatenate([_pad_lanes(cache_kpe, LANES).astype(BF16),
                                   kpe.astype(BF16).reshape(b, t, LANES)], axis=1)
        length = ckv_all.shape[1]
        s = t
    else:
        s = seq
        b = m // s
        ckv_all, kpe_all, length = ckv.astype(BF16), kpe.astype(BF16), s
    k, v = _mla_ukv(ckv_all.reshape(b * length, kvl), w_ukv.astype(BF16), kpe_all.reshape(b * length, LANES))
    o = _attention(q.reshape(b, s, H_A * 2 * LANES), k.reshape(b, length, H_A * 2 * LANES),
                   v.reshape(b, length, H_A * V_A), kv_heads=H_A, reps=1, dq=2 * LANES, dv=V_A)
    return o, ckv, kpe


def _diff_mixer(x, mod, g, groups, t, sample, seq, cache_k, cache_v, w_qkv, lam_vecs, subln, lam_init):
    m = x.shape[0]
    d_b = H_B * 2 * DH_B
    tables = _rope_tables(t, DH_B, 0, 0, reps=2) if sample else None
    outs = _qkv_proj(x, mod, g, w_qkv.astype(BF16), groups, tn=min(1024, d_b), n_q=d_b // min(1024, d_b),
                     n_k=d_b // min(1024, d_b), pattern=2 * DH_B, half=DH_B // 4,
                     q_scale=1.0 / math.sqrt(DH_B), tables=tables, want_f32=not sample)
    qkv = outs[0]
    hb = d_b // (2 * DH_B)
    if sample:
        b = groups
        qkv = qkv.reshape(b, t, 3 * d_b)
        past = cache_k.shape[1]
        k = jnp.concatenate([cache_k.reshape(b, past, d_b).astype(BF16), qkv[:, :, d_b:2 * d_b]], axis=1)
        v = jnp.concatenate([cache_v.reshape(b, past, d_b).astype(BF16), qkv[:, :, 2 * d_b:]], axis=1)
        o = _attention(qkv, k, v, kv_heads=H_B, reps=1, dq=2 * DH_B, dv=2 * DH_B, diff=True, lam=lam_vecs,
                       subln=subln[None], lam_init=lam_init)
        return o, None
    b = m // seq
    qkv = qkv.reshape(b, seq, 3 * d_b)
    o = _attention(qkv, qkv, qkv, kv_heads=H_B, reps=1, dq=2 * DH_B, dv=2 * DH_B, diff=True, lam=lam_vecs,
                   subln=subln[None], lam_init=lam_init, k_off=hb, v_off=2 * hb)
    return o, outs[1]


def _gqa_mixer(x, mod, g, groups, t, sample, seq, cache_k, cache_v, w_qkv, q_norm, k_norm):
    m = x.shape[0]
    dq_all, dk_all = H_C * HD_C, KVH_C * HD_C
    tn = dk_all
    tables = _rope_tables(t, HD_C, 0, 0) if sample else None
    outs = _qkv_proj(x, mod, g, w_qkv.astype(BF16), groups, tn=tn, n_q=dq_all // tn, n_k=1, pattern=HD_C,
                     half=HD_C // 4, q_scale=1.0 / math.sqrt(HD_C), tables=tables,
                     norms=(q_norm[None], k_norm[None]), want_f32=not sample)
    qkv = outs[0]
    reps = H_C // KVH_C
    if sample:
        b = groups
        qkv = qkv.reshape(b, t, dq_all + 2 * dk_all)
        past = cache_k.shape[1]
        k = jnp.concatenate([cache_k.reshape(b, past, dk_all).astype(BF16),
                             qkv[:, :, dq_all:dq_all + dk_all]], axis=1)
        v = jnp.concatenate([cache_v.reshape(b, past, dk_all).astype(BF16), qkv[:, :, dq_all + dk_all:]], axis=1)
        o = _attention(qkv, k, v, kv_heads=KVH_C, reps=reps, dq=HD_C, dv=HD_C)
        return o, None
    b = m // seq
    qkv = qkv.reshape(b, seq, dq_all + 2 * dk_all)
    o = _attention(qkv, qkv, qkv, kv_heads=KVH_C, reps=reps, dq=HD_C, dv=HD_C, k_off=H_C, v_off=H_C + KVH_C)
    return o, outs[1]


def _lambda_init(layer):
    return 0.8 - 0.6 * math.exp(-0.3 * layer)


def kernel(x_prompt, x_sample, c, cache_a_ckv, cache_a_kpe, cache_b_k, cache_b_v, cache_c_k, cache_c_v, c_ctx, norm_g, w_mod, b_mod, ffn_w_in, ffn_conv_w, ffn_conv_b, ffn_w_down, a_w_down, a_q_norm, a_kv_norm, a_w_uq, a_w_ukv, a_w_o, b_w_qkv, b_lambda, b_subln, b_w_o, c_w_qkv, c_q_norm, c_k_norm, c_w_o):
    batch, seq, d = x_prompt.shape
    dec_batch, dec_seq, _ = x_sample.shape
    depth = w_mod.shape[0]
    tp = batch * seq

    cvec = jnp.concatenate([c, c_ctx[None], jnp.zeros((8 - dec_batch - 1, d), F32)], axis=0)
    mods = _modulation(cvec, w_mod, b_mod)

    xp = x_prompt.reshape(tp, d)
    xs = x_sample.reshape(dec_batch * dec_seq, d)
    kv_lora = a_kv_norm.shape[1]
    st = [[] for _ in range(6)]

    for l in range(depth):
        kind, j = l % N_MIXERS, l // N_MIXERS
        mod_s = mods[l, :dec_batch].reshape(dec_batch, 6, d)
        mod_p = mods[l, dec_batch:dec_batch + 1].reshape(1, 6, d)
        g = norm_g[l]
        streams = ((xp, mod_p, 1, tp, False, seq), (xs, mod_s, dec_batch, dec_seq, True, dec_seq))
        new = []
        for x, mod, groups, t, sample, sq in streams:
            if kind == 0:
                o, ckv, kpe = _mla_mixer(x, mod, g[0:1], groups, t, sample, sq, cache_a_ckv[:, j], cache_a_kpe[:, j],
                                         a_w_down[j], a_q_norm[j], a_kv_norm[j], a_w_uq[j], a_w_ukv[j])
                if not sample:
                    st[0].append(ckv.reshape(batch, seq, kv_lora))
                    st[1].append(kpe[:, :ROPE_A].reshape(batch, seq, ROPE_A))
                w_o = a_w_o[j]
            elif kind == 1:
                o, full = _diff_mixer(x, mod, g[0:1], groups, t, sample, sq, cache_b_k[:, j], cache_b_v[:, j],
                                      b_w_qkv[j], b_lambda[j], b_subln[j], _lambda_init(l))
                if not sample:
                    d_b = H_B * 2 * DH_B
                    st[2].append(full[:, d_b:2 * d_b].reshape(batch, seq, H_B, 2, DH_B))
                    st[3].append(full[:, 2 * d_b:].reshape(batch, seq, H_B, 2 * DH_B))
                w_o = b_w_o[j]
            else:
                o, full = _gqa_mixer(x, mod, g[0:1], groups, t, sample, sq, cache_c_k[:, j], cache_c_v[:, j],
                                     c_w_qkv[j], c_q_norm[j], c_k_norm[j])
                if not sample:
                    dq_all, dk_all = H_C * HD_C, KVH_C * HD_C
                    st[4].append(full[:, dq_all:dq_all + dk_all].reshape(batch, seq, KVH_C, HD_C))
                    st[5].append(full[:, dq_all + dk_all:].reshape(batch, seq, KVH_C, HD_C))
                w_o = c_w_o[j]
            x = _out_proj(o.reshape(x.shape[0], -1), w_o.astype(BF16), x, mod, g[1:2], groups, 2)
            x = _conv_ffn(x, mod, g[2:3], g[3:4], ffn_w_in[l].astype(BF16), ffn_conv_w[l], ffn_conv_b[l][None],
                          ffn_w_down[l].astype(BF16), groups, sq)
            new.append(x)
        xp, xs = new

    states = tuple(jnp.stack(s, axis=1) for s in st)
    return (xp.reshape(batch, seq, d), xs.reshape(dec_batch, dec_seq, d)) + states
```

```python
import functools
import math

import jax
import jax.numpy as jnp
import numpy as np
from jax import lax
from jax.experimental import pallas as pl
from jax.experimental.pallas import tpu as pltpu

F32 = jnp.float32
BF16 = jnp.bfloat16

EPS = 1e-6
LOG2E = math.log2(math.e)
ROPE_BASE = 10000.0
GRID_W = 64
N_MIXERS = 3
H_A = 16
NOPE_A = 128
ROPE_A = 64
V_A = 128
H_B = 16
DH_B = 64
H_C = 16
KVH_C = 4
HD_C = 128

LANES = 128
VMEM_LIMIT = 56 * 1024 * 1024
ROW_TILE = 512
ATTN_ROWS = 1024
KV_CHUNK = 1536


def _params(*sem):
    return pltpu.CompilerParams(dimension_semantics=sem, vmem_limit_bytes=VMEM_LIMIT)


def _largest_tile(n, cap):
    for tile in range(cap, 0, -LANES):
        if n % tile == 0:
            return tile
    return n


def _dot(a, b):
    return jnp.dot(a, b, preferred_element_type=F32)


def _dot_nt(a, b):
    return lax.dot_general(a, b, (((1,), (1,)), ((), ())), preferred_element_type=F32)


def _rms(x, g):
    return x * lax.rsqrt(jnp.mean(x * x, axis=-1, keepdims=True) + EPS) * g


def _modnorm(x, g, shift, scale):
    return _rms(x, g) * (1.0 + scale) + shift


def _rope(x, cos, sin, half):
    w = x.shape[-1]
    lane = lax.broadcasted_iota(jnp.int32, x.shape, 1)
    first = (lane & (2 * half - 1)) < half
    partner = jnp.where(first, pltpu.roll(x, w - half, 1), pltpu.roll(x, half, 1))
    return x * cos + partner * sin


def _rope_base(t, dim):
    pos = jnp.arange(t)
    row = (pos // GRID_W).astype(F32)
    col = (pos % GRID_W).astype(F32)
    r = dim // 2
    half = r // 2
    freqs = ROPE_BASE ** (-jnp.arange(half, dtype=F32) / half)
    lane = np.arange(dim)
    idx = (lane % r) % half
    first = (lane % r) < half
    use_col = (lane // r) == 1
    ang = jnp.where(use_col[None, :], col[:, None], row[:, None]) * freqs[idx][None, :]
    return jnp.cos(ang), jnp.where(first[None, :], -jnp.sin(ang), jnp.sin(ang))


def _rope_tables(t, dim, left, right, reps=1, scale=1.0):
    cos, sin = _rope_base(t, dim)
    cos = jnp.concatenate([jnp.ones((t, left), F32)] + [cos] * reps + [jnp.ones((t, right), F32)], axis=1)
    sin = jnp.concatenate([jnp.zeros((t, left), F32)] + [sin] * reps + [jnp.zeros((t, right), F32)], axis=1)
    return cos * scale, sin * scale


def _mod_kernel(c_ref, w_ref, b_ref, o_ref):
    c = c_ref[...]
    s = c * (1.0 / (1.0 + jnp.exp(-c)))
    o_ref[0] = _dot(s.astype(BF16), w_ref[0].astype(BF16)) + b_ref[0]


def _modulation(cvec, w_mod, b_mod):
    depth, d, n = w_mod.shape
    tn = _largest_tile(n, 1024)
    return pl.pallas_call(
        _mod_kernel,
        out_shape=jax.ShapeDtypeStruct((depth, cvec.shape[0], n), F32),
        grid=(depth, n // tn),
        in_specs=[pl.BlockSpec(cvec.shape, lambda l, j: (0, 0)),
                  pl.BlockSpec((1, d, tn), lambda l, j: (l, 0, j)),
                  pl.BlockSpec((1, 1, tn), lambda l, j: (l, 0, j))],
        out_specs=pl.BlockSpec((1, cvec.shape[0], tn), lambda l, j: (l, 0, j)),
        compiler_params=_params("arbitrary", "arbitrary"),
        name="adaln_modulation",
    )(cvec, w_mod, b_mod.reshape(depth, 1, n))


def _tile_rows(t):
    return min(ROW_TILE, t)


def _mla_down_kernel(*refs, ql, kvl, rope):
    if rope:
        x_ref, mod_ref, g_ref, w_ref, qn_ref, kvn_ref, cos_ref, sin_ref, cq_ref, ckv_ref, kpe_ref = refs
    else:
        x_ref, mod_ref, g_ref, w_ref, qn_ref, kvn_ref, cq_ref, ckv_ref, kpe_ref = refs
    h = _modnorm(x_ref[...], g_ref[...], mod_ref[0, 0:1, :], mod_ref[0, 1:2, :])
    d = _dot(h.astype(BF16), w_ref[...])
    cq_ref[...] = _rms(d[:, :ql], qn_ref[...]).astype(BF16)
    ckv_ref[...] = _rms(d[:, ql:ql + kvl], kvn_ref[...])
    kpe = d[:, ql + kvl:]
    if rope:
        kpe = _rope(kpe, cos_ref[...], sin_ref[...], ROPE_A // 4)
    kpe_ref[...] = kpe


def _mla_down(x, mod, g, w, qn, kvn, tables, groups):
    m, d = x.shape
    t = m // groups
    tm = _tile_rows(t)
    nt = t // tm
    ql, kvl = qn.shape[1], kvn.shape[1]
    n = w.shape[1]
    rope = tables is not None
    row = lambda gi, i: (gi * nt + i, 0)
    const = lambda gi, i: (0, 0)
    in_specs = [pl.BlockSpec((tm, d), row),
                pl.BlockSpec((1, 6, d), lambda gi, i: (gi, 0, 0)),
                pl.BlockSpec((1, d), const),
                pl.BlockSpec((d, n), const),
                pl.BlockSpec((1, ql), const),
                pl.BlockSpec((1, kvl), const)]
    args = [x, mod, g, w, qn, kvn]
    if rope:
        in_specs += [pl.BlockSpec((tm, LANES), lambda gi, i: (i, 0))] * 2
        args += list(tables)
    return pl.pallas_call(
        functools.partial(_mla_down_kernel, ql=ql, kvl=kvl, rope=rope),
        out_shape=(jax.ShapeDtypeStruct((m, ql), BF16),
                   jax.ShapeDtypeStruct((m, kvl), F32),
                   jax.ShapeDtypeStruct((m, n - ql - kvl), F32)),
        grid=(groups, nt),
        in_specs=in_specs,
        out_specs=(pl.BlockSpec((tm, ql), row), pl.BlockSpec((tm, kvl), row),
                   pl.BlockSpec((tm, n - ql - kvl), row)),
        compiler_params=_params("arbitrary", "arbitrary"),
        name="mla_down_proj",
    )(*args)


def _qkv_kernel(*refs, rope, want_f32, n_q, n_k, pattern, half, qk_norm, q_scale):
    refs = list(refs)
    x_ref, mod_ref, g_ref, w_ref = refs[:4]
    pos = 4
    if qk_norm:
        qn_ref, kn_ref = refs[pos:pos + 2]
        pos += 2
    if rope:
        cos_ref, sin_ref = refs[pos:pos + 2]
        pos += 2
    ob_ref = refs[pos]
    pos += 1
    if want_f32:
        of_ref = refs[pos]
        pos += 1
    h_ref = refs[pos]
    j = pl.program_id(2)

    @pl.when(j == 0)
    def _():
        h_ref[...] = _modnorm(x_ref[...], g_ref[...], mod_ref[0, 0:1, :], mod_ref[0, 1:2, :]).astype(BF16)

    acc = _dot(h_ref[...], w_ref[...])
    tn = acc.shape[1]

    def emit(kind):
        for c in range(tn // pattern):
            sl = slice(c * pattern, (c + 1) * pattern)
            y = acc[:, sl]
            if kind != "v":
                if qk_norm:
                    y = _rms(y, qn_ref[...] if kind == "q" else kn_ref[...])
                if want_f32:
                    of_ref[:, sl] = y
                if rope:
                    y = _rope(y, cos_ref[...], sin_ref[...], half)
                if kind == "q":
                    y = y * q_scale
            elif want_f32:
                of_ref[:, sl] = y
            ob_ref[:, sl] = y.astype(BF16)

    pl.when(j < n_q)(lambda: emit("q"))
    pl.when((j >= n_q) & (j < n_q + n_k))(lambda: emit("k"))
    pl.when(j >= n_q + n_k)(lambda: emit("v"))


def _qkv_proj(x, mod, g, w, groups, *, tn, n_q, n_k, pattern, half, q_scale, tables=None, norms=None,
              want_f32=False):
    m, d = x.shape
    n = w.shape[1]
    t = m // groups
    tm = _tile_rows(t)
    nt = t // tm
    rope = tables is not None
    qk_norm = norms is not None
    row = lambda gi, i, j: (gi * nt + i, 0)
    const = lambda gi, i, j: (0, 0)
    tile = lambda gi, i, j: (gi * nt + i, j)
    in_specs = [pl.BlockSpec((tm, d), row),
                pl.BlockSpec((1, 6, d), lambda gi, i, j: (gi, 0, 0)),
                pl.BlockSpec((1, d), const),
                pl.BlockSpec((d, tn), lambda gi, i, j: (0, j))]
    args = [x, mod, g, w]
    if qk_norm:
        in_specs += [pl.BlockSpec((1, pattern), const)] * 2
        args += list(norms)
    if rope:
        in_specs += [pl.BlockSpec((tm, pattern), lambda gi, i, j: (i, 0))] * 2
        args += list(tables)
    out_shape = [jax.ShapeDtypeStruct((m, n), BF16)]
    out_specs = [pl.BlockSpec((tm, tn), tile)]
    if want_f32:
        out_shape.append(jax.ShapeDtypeStruct((m, n), F32))
        out_specs.append(pl.BlockSpec((tm, tn), tile))
    return pl.pallas_call(
        functools.partial(_qkv_kernel, rope=rope, want_f32=want_f32, n_q=n_q, n_k=n_k, pattern=pattern,
                          half=half, qk_norm=qk_norm, q_scale=q_scale),
        out_shape=tuple(out_shape),
        grid=(groups, nt, n // tn),
        in_specs=in_specs,
        out_specs=tuple(out_specs),
        scratch_shapes=[pltpu.VMEM((tm, d), BF16)],
        compiler_params=_params("arbitrary", "arbitrary", "arbitrary"),
        name="qkv_proj",
    )(*args)


def _mla_uq_kernel(*refs, rope, pattern, scale):
    if rope:
        x_ref, w_ref, cos_ref, sin_ref, o_ref = refs
    else:
        x_ref, w_ref, o_ref = refs
    acc = _dot(x_ref[...], w_ref[...])
    for c in range(acc.shape[1] // pattern):
        sl = slice(c * pattern, (c + 1) * pattern)
        y = acc[:, sl]
        if rope:
            y = _rope(y, cos_ref[...], sin_ref[...], ROPE_A // 4)
        o_ref[:, sl] = (y * scale).astype(BF16)


def _mla_uq(cq, w, tables, t, scale):
    m, k = cq.shape
    n = w.shape[1]
    tm = _tile_rows(t)
    nt = t // tm
    tn = min(1024, n)
    pattern = 2 * LANES
    rope = tables is not None
    in_specs = [pl.BlockSpec((tm, k), lambda i, j: (i, 0)), pl.BlockSpec((k, tn), lambda i, j: (0, j))]
    args = [cq, w]
    if rope:
        in_specs += [pl.BlockSpec((tm, pattern), lambda i, j: (i % nt, 0))] * 2
        args += list(tables)
    return pl.pallas_call(
        functools.partial(_mla_uq_kernel, rope=rope, pattern=pattern, scale=scale),
        out_shape=jax.ShapeDtypeStruct((m, n), BF16),
        grid=(m // tm, n // tn),
        in_specs=in_specs,
        out_specs=pl.BlockSpec((tm, tn), lambda i, j: (i, j)),
        compiler_params=_params("arbitrary", "arbitrary"),
        name="mla_q_up_proj",
    )(*args)


def _mla_ukv_kernel(x_ref, w_ref, kpe_ref, k_ref, v_ref, *, heads):
    acc = _dot(x_ref[...], w_ref[...])
    kpe = kpe_ref[...]
    for h in range(heads):
        k_ref[:, 2 * h * LANES:(2 * h + 1) * LANES] = acc[:, 2 * h * LANES:(2 * h + 1) * LANES].astype(BF16)
        k_ref[:, (2 * h + 1) * LANES:(2 * h + 2) * LANES] = kpe
        v_ref[:, h * LANES:(h + 1) * LANES] = acc[:, (2 * h + 1) * LANES:(2 * h + 2) * LANES].astype(BF16)


def _mla_ukv(ckv, w, kpe):
    m, k = ckv.shape
    n = w.shape[1]
    tm = 512 if m % 512 == 0 else m
    heads = min(4, H_A)
    tn = heads * 2 * LANES
    return pl.pallas_call(
        functools.partial(_mla_ukv_kernel, heads=heads),
        out_shape=(jax.ShapeDtypeStruct((m, n), BF16), jax.ShapeDtypeStruct((m, n // 2), BF16)),
        grid=(m // tm, n // tn),
        in_specs=[pl.BlockSpec((tm, k), lambda i, j: (i, 0)),
                  pl.BlockSpec((k, tn), lambda i, j: (0, j)),
                  pl.BlockSpec((tm, LANES), lambda i, j: (i, 0))],
        out_specs=(pl.BlockSpec((tm, tn), lambda i, j: (i, j)),
                   pl.BlockSpec((tm, tn // 2), lambda i, j: (i, j))),
        compiler_params=_params("arbitrary", "arbitrary"),
        name="mla_kv_up_proj",
    )(ckv, w, kpe)


def _attn_kernel(*refs, diff, reps, dq, dv, tq, length, tk, lam_init):
    if diff:
        q_ref, k_ref, v_ref, lam_ref, subln_ref, o_ref, vt_ref = refs
    else:
        q_ref, k_ref, v_ref, o_ref, vt_ref = refs

    @pl.when(pl.program_id(2) == 0)
    def _():
        vt_ref[...] = v_ref[0].T

    q = q_ref[0]
    if diff:
        lane = lax.broadcasted_iota(jnp.int32, q.shape, 1)
        zero = jnp.zeros_like(q)
        qs = jnp.concatenate([jnp.where(lane < dq // 2, q, zero), jnp.where(lane >= dq // 2, q, zero)], axis=0)
    elif reps > 1:
        qs = jnp.concatenate([q[:, r * dq:(r + 1) * dq] for r in range(reps)], axis=0)
    else:
        qs = q
    def scores(c):
        return _dot_nt(k_ref[0, c * tk:(c + 1) * tk, :], qs)

    nc = length // tk
    m = l = acc = None
    st_next = scores(0)
    for c in range(nc):
        st, st_next = st_next, (scores(c + 1) if c + 1 < nc else None)
        cmax = jnp.max(st, axis=0, keepdims=True)
        m_new = cmax if c == 0 else jnp.maximum(m, cmax)
        p = jnp.exp2(st - m_new)
        psum = jnp.sum(p, axis=0, keepdims=True)
        pv = _dot(vt_ref[:, c * tk:(c + 1) * tk], p.astype(BF16))
        if c == 0:
            l, acc = psum, pv
        else:
            alpha = jnp.exp2(m - m_new)
            l = alpha * l + psum
            acc = alpha * acc + pv
        m = m_new
    if diff:
        lv = lam_ref[...]
        lam = (jnp.exp(jnp.sum(lv[0:1] * lv[1:2], axis=-1, keepdims=True))
               - jnp.exp(jnp.sum(lv[2:3] * lv[3:4], axis=-1, keepdims=True)) + lam_init)
        ot = acc[:, :tq] / l[:, :tq] - lam * (acc[:, tq:] / l[:, tq:])
        o_ref[0] = (_rms(ot.T, subln_ref[...]) * (1.0 - lam_init)).astype(BF16)
    else:
        o = (acc / l).T
        for r in range(reps):
            o_ref[0, :, r * dv:(r + 1) * dv] = o[r * tq:(r + 1) * tq].astype(BF16)


def _attention(q, k, v, *, kv_heads, reps, dq, dv, diff=False, lam=None, subln=None, lam_init=0.0,
               q_off=0, k_off=0, v_off=0):
    b, s, _ = q.shape
    length = k.shape[1]
    stack = 2 if diff else reps
    tq = min(ATTN_ROWS // stack, s)
    tk = min(KV_CHUNK, length)
    in_specs = [pl.BlockSpec((1, tq, reps * dq), lambda bi, h, i: (bi, i, q_off + h)),
                pl.BlockSpec((1, length, dq), lambda bi, h, i: (bi, 0, k_off + h)),
                pl.BlockSpec((1, length, dv), lambda bi, h, i: (bi, 0, v_off + h))]
    args = [q, k, v]
    if diff:
        in_specs += [pl.BlockSpec(lam.shape, lambda bi, h, i: (0, 0)),
                     pl.BlockSpec(subln.shape, lambda bi, h, i: (0, 0))]
        args += [lam, subln]
    return pl.pallas_call(
        functools.partial(_attn_kernel, diff=diff, reps=reps, dq=dq, dv=dv, tq=tq, length=length, tk=tk,
                          lam_init=lam_init),
        out_shape=jax.ShapeDtypeStruct((b, s, kv_heads * reps * dv), BF16),
        grid=(b, kv_heads, s // tq),
        in_specs=in_specs,
        out_specs=pl.BlockSpec((1, tq, reps * dv), lambda bi, h, i: (bi, i, h)),
        scratch_shapes=[pltpu.VMEM((dv, length), BF16)],
        compiler_params=_params("arbitrary", "arbitrary", "arbitrary"),
        name="diff_attention" if diff else "attention",
    )(*args)


def _out_proj_kernel(o_ref, w_ref, x_ref, mod_ref, g_ref, y_ref, *, gate_row):
    f = _dot(o_ref[...], w_ref[...])
    y_ref[...] = x_ref[...] + mod_ref[0, gate_row:gate_row + 1, :] * _rms(f, g_ref[...])


def _out_proj(o, w, x, mod, g, groups, gate_row):
    m, d = x.shape
    k = o.shape[1]
    t = m // groups
    tm = _tile_rows(t)
    nt = t // tm
    row = lambda gi, i: (gi * nt + i, 0)
    const = lambda gi, i: (0, 0)
    return pl.pallas_call(
        functools.partial(_out_proj_kernel, gate_row=gate_row),
        out_shape=jax.ShapeDtypeStruct((m, d), F32),
        grid=(groups, nt),
        in_specs=[pl.BlockSpec((tm, k), row),
                  pl.BlockSpec((k, d), const),
                  pl.BlockSpec((tm, d), row),
                  pl.BlockSpec((1, 6, d), lambda gi, i: (gi, 0, 0)),
                  pl.BlockSpec((1, d), const)],
        out_specs=pl.BlockSpec((tm, d), row),
        compiler_params=_params("arbitrary", "arbitrary"),
        name="out_proj_residual",
    )(o, w, x, mod, g)


HALO = 8


def _ffn_kernel(x_ref, xp_ref, xn_ref, mod_ref, g_ref, go_ref, wg_ref, wv_ref, cwg_ref, cwv_ref, cbg_ref,
                cbv_ref, wd_ref, y_ref, h_ref, act_ref, acc_ref, *, tm, seq, nf):
    i = pl.program_id(1)
    j = pl.program_id(2)
    slot = j % 2

    @pl.when(j == 0)
    def _():
        shift, scale = mod_ref[0, 3:4, :], mod_ref[0, 4:5, :]
        h_ref[0:tm, :] = _modnorm(x_ref[...], g_ref[...], shift, scale).astype(BF16)
        halo = jnp.concatenate([xp_ref[...], xn_ref[...]], axis=0)
        h_ref[tm:tm + 2 * HALO, :] = _modnorm(halo, g_ref[...], shift, scale).astype(BF16)
        acc_ref[...] = jnp.zeros_like(acc_ref)

    r = lax.broadcasted_iota(jnp.int32, (tm, 1), 0)
    tok = i * tm + r
    seq_first = (tok % seq) == 0
    seq_last = (tok % seq) == seq - 1

    def conv(u, cw_ref, cb_ref):
        um = u[0:tm]
        prev_row = u[tm + HALO - 1:tm + HALO]
        next_row = u[tm + HALO:tm + HALO + 1]
        up = jnp.where(r == 0, prev_row, pltpu.roll(um, 1, 0))
        un = jnp.where(r == tm - 1, next_row, pltpu.roll(um, tm - 1, 0))
        up = jnp.where(seq_first, 0.0, up)
        un = jnp.where(seq_last, 0.0, un)
        return up * cw_ref[0:1, :] + um * cw_ref[1:2, :] + un * cw_ref[2:3, :] + cb_ref[...]

    def step(up, down):
        if up:
            ug = _dot(h_ref[...], wg_ref[...])
            uv = _dot(h_ref[...], wv_ref[...])
        if down:
            acc_ref[...] += _dot(act_ref[1 - slot], wd_ref[...])
        if up:
            gate = conv(ug, cwg_ref, cbg_ref)
            val = conv(uv, cwv_ref, cbv_ref)
            act_ref[slot] = (gate * (1.0 / (1.0 + jnp.exp(-gate))) * val).astype(BF16)

    pl.when(j == 0)(lambda: step(True, False))
    pl.when((j > 0) & (j < nf))(lambda: step(True, True))

    @pl.when(j == nf)
    def _():
        step(False, True)
        y_ref[...] = x_ref[...] + mod_ref[0, 5:6, :] * _rms(acc_ref[...], go_ref[...])


def _conv_ffn(x, mod, g_in, g_out, w_in, conv_w, conv_b, w_down, groups, seq):
    m, d = x.shape
    ff = w_down.shape[0]
    t = m // groups
    tm = _tile_rows(t)
    nt = t // tm
    tf = 512 if ff % 512 == 0 else ff
    nf = ff // tf
    hb = tm // HALO
    last_halo = m // HALO - 1
    row = lambda gi, i, j: (gi * nt + i, 0)
    const = lambda gi, i, j: (0, 0)
    up = lambda j: jnp.minimum(j, nf - 1)
    down = lambda j: jnp.maximum(j - 1, 0)
    return pl.pallas_call(
        functools.partial(_ffn_kernel, tm=tm, seq=seq, nf=nf),
        out_shape=jax.ShapeDtypeStruct((m, d), F32),
        grid=(groups, nt, nf + 1),
        in_specs=[pl.BlockSpec((tm, d), row),
                  pl.BlockSpec((HALO, d), lambda gi, i, j: (jnp.maximum((gi * nt + i) * hb - 1, 0), 0)),
                  pl.BlockSpec((HALO, d), lambda gi, i, j: (jnp.minimum((gi * nt + i + 1) * hb, last_halo), 0)),
                  pl.BlockSpec((1, 6, d), lambda gi, i, j: (gi, 0, 0)),
                  pl.BlockSpec((1, d), const),
                  pl.BlockSpec((1, d), const),
                  pl.BlockSpec((d, tf), lambda gi, i, j: (0, up(j))),
                  pl.BlockSpec((d, tf), lambda gi, i, j: (0, nf + up(j))),
                  pl.BlockSpec((3, tf), lambda gi, i, j: (0, up(j))),
                  pl.BlockSpec((3, tf), lambda gi, i, j: (0, nf + up(j))),
                  pl.BlockSpec((1, tf), lambda gi, i, j: (0, up(j))),
                  pl.BlockSpec((1, tf), lambda gi, i, j: (0, nf + up(j))),
                  pl.BlockSpec((tf, d), lambda gi, i, j: (down(j), 0))],
        out_specs=pl.BlockSpec((tm, d), row),
        scratch_shapes=[pltpu.VMEM((tm + 2 * HALO, d), BF16), pltpu.VMEM((2, tm, tf), BF16),
                        pltpu.VMEM((tm, d), F32)],
        compiler_params=_params("arbitrary", "arbitrary", "arbitrary"),
        name="conv_ffn",
    )(x, x, x, mod, g_in, g_out, w_in, w_in, conv_w, conv_w, conv_b, conv_b, w_down)


def _pad_lanes(a, width):
    return jnp.pad(a, [(0, 0)] * (a.ndim - 1) + [(0, width - a.shape[-1])])


def _mla_mixer(x, mod, g, groups, t, sample, seq, cache_ckv, cache_kpe, w_down, q_norm, kv_norm, w_uq, w_ukv):
    m = x.shape[0]
    ql, kvl = q_norm.shape[0], kv_norm.shape[0]
    scale = LOG2E / math.sqrt(NOPE_A + ROPE_A)
    kpe_tables = _rope_tables(t, ROPE_A, 0, LANES - ROPE_A) if sample else None
    q_tables = _rope_tables(t, ROPE_A, NOPE_A, 2 * LANES - NOPE_A - ROPE_A) if sample else None
    w_down_p = _pad_lanes(w_down, ql + kvl + LANES).astype(BF16)
    cq, ckv, kpe = _mla_down(x, mod, g, w_down_p, q_norm[None], kv_norm[None], kpe_tables, groups)
    w_uq_p = _pad_lanes(w_uq.reshape(ql, H_A, NOPE_A + ROPE_A), 2 * LANES).reshape(ql, H_A * 2 * LANES)
    q = _mla_uq(cq, w_uq_p.astype(BF16), q_tables, t, scale)
    if sample:
        b = groups
        ckv_all = jnp.concatenate([cache_ckv.astype(BF16), ckv.astype(BF16).reshape(b, t, kvl)], axis=1)
        kpe_all = jnp.concatenate([_pad_lanes(cache_kpe, LANES).astype(BF16),
                                   kpe.astype(BF16).reshape(b, t, LANES)], axis=1)
        length = ckv_all.shape[1]
        s = t
    else:
        s = seq
        b = m // s
        ckv_all, kpe_all, length = ckv.astype(BF16), kpe.astype(BF16), s
    k, v = _mla_ukv(ckv_all.reshape(b * length, kvl), w_ukv.astype(BF16), kpe_all.reshape(b * length, LANES))
    o = _attention(q.reshape(b, s, H_A * 2 * LANES), k.reshape(b, length, H_A * 2 * LANES),
                   v.reshape(b, length, H_A * V_A), kv_heads=H_A, reps=1, dq=2 * LANES, dv=V_A)
    return o, ckv, kpe


def _diff_mixer(x, mod, g, groups, t, sample, seq, cache_k, cache_v, w_qkv, lam_vecs, subln, lam_init):
    m = x.shape[0]
    d_b = H_B * 2 * DH_B
    tables = _rope_tables(t, DH_B, 0, 0, reps=2) if sample else None
    outs = _qkv_proj(x, mod, g, w_qkv.astype(BF16), groups, tn=min(1024, d_b), n_q=d_b // min(1024, d_b),
                     n_k=d_b // min(1024, d_b), pattern=2 * DH_B, half=DH_B // 4,
                     q_scale=LOG2E / math.sqrt(DH_B), tables=tables, want_f32=not sample)
    qkv = outs[0]
    hb = d_b // (2 * DH_B)
    if sample:
        b = groups
        qkv = qkv.reshape(b, t, 3 * d_b)
        past = cache_k.shape[1]
        k = jnp.concatenate([cache_k.reshape(b, past, d_b).astype(BF16), qkv[:, :, d_b:2 * d_b]], axis=1)
        v = jnp.concatenate([cache_v.reshape(b, past, d_b).astype(BF16), qkv[:, :, 2 * d_b:]], axis=1)
        o = _attention(qkv, k, v, kv_heads=H_B, reps=1, dq=2 * DH_B, dv=2 * DH_B, diff=True, lam=lam_vecs,
                       subln=subln[None], lam_init=lam_init)
        return o, None
    b = m // seq
    qkv = qkv.reshape(b, seq, 3 * d_b)
    o = _attention(qkv, qkv, qkv, kv_heads=H_B, reps=1, dq=2 * DH_B, dv=2 * DH_B, diff=True, lam=lam_vecs,
                   subln=subln[None], lam_init=lam_init, k_off=hb, v_off=2 * hb)
    return o, outs[1]


def _gqa_mixer(x, mod, g, groups, t, sample, seq, cache_k, cache_v, w_qkv, q_norm, k_norm):
    m = x.shape[0]
    dq_all, dk_all = H_C * HD_C, KVH_C * HD_C
    tn = dk_all
    tables = _rope_tables(t, HD_C, 0, 0) if sample else None
    outs = _qkv_proj(x, mod, g, w_qkv.astype(BF16), groups, tn=tn, n_q=dq_all // tn, n_k=1, pattern=HD_C,
                     half=HD_C // 4, q_scale=LOG2E / math.sqrt(HD_C), tables=tables,
                     norms=(q_norm[None], k_norm[None]), want_f32=not sample)
    qkv = outs[0]
    reps = H_C // KVH_C
    if sample:
        b = groups
        qkv = qkv.reshape(b, t, dq_all + 2 * dk_all)
        past = cache_k.shape[1]
        k = jnp.concatenate([cache_k.reshape(b, past, dk_all).astype(BF16),
                             qkv[:, :, dq_all:dq_all + dk_all]], axis=1)
        v = jnp.concatenate([cache_v.reshape(b, past, dk_all).astype(BF16), qkv[:, :, dq_all + dk_all:]], axis=1)
        o = _attention(qkv, k, v, kv_heads=KVH_C, reps=reps, dq=HD_C, dv=HD_C)
        return o, None
    b = m // seq
    qkv = qkv.reshape(b, seq, dq_all + 2 * dk_all)
    o = _attention(qkv, qkv, qkv, kv_heads=KVH_C, reps=reps, dq=HD_C, dv=HD_C, k_off=H_C, v_off=H_C + KVH_C)
    return o, outs[1]


def _lambda_init(layer):
    return 0.8 - 0.6 * math.exp(-0.3 * layer)


def kernel(x_prompt, x_sample, c, cache_a_ckv, cache_a_kpe, cache_b_k, cache_b_v, cache_c_k, cache_c_v, c_ctx, norm_g, w_mod, b_mod, ffn_w_in, ffn_conv_w, ffn_conv_b, ffn_w_down, a_w_down, a_q_norm, a_kv_norm, a_w_uq, a_w_ukv, a_w_o, b_w_qkv, b_lambda, b_subln, b_w_o, c_w_qkv, c_q_norm, c_k_norm, c_w_o):
    batch, seq, d = x_prompt.shape
    dec_batch, dec_seq, _ = x_sample.shape
    depth = w_mod.shape[0]
    tp = batch * seq

    cvec = jnp.concatenate([c, c_ctx[None], jnp.zeros((8 - dec_batch - 1, d), F32)], axis=0)
    mods = _modulation(cvec, w_mod, b_mod)

    xp = x_prompt.reshape(tp, d)
    xs = x_sample.reshape(dec_batch * dec_seq, d)
    kv_lora = a_kv_norm.shape[1]
    st = [[] for _ in range(6)]

    for l in range(depth):
        kind, j = l % N_MIXERS, l // N_MIXERS
        mod_s = mods[l, :dec_batch].reshape(dec_batch, 6, d)
        mod_p = mods[l, dec_batch:dec_batch + 1].reshape(1, 6, d)
        g = norm_g[l]
        streams = ((xp, mod_p, 1, tp, False, seq), (xs, mod_s, dec_batch, dec_seq, True, dec_seq))
        new = []
        for x, mod, groups, t, sample, sq in streams:
            if kind == 0:
                o, ckv, kpe = _mla_mixer(x, mod, g[0:1], groups, t, sample, sq, cache_a_ckv[:, j], cache_a_kpe[:, j],
                                         a_w_down[j], a_q_norm[j], a_kv_norm[j], a_w_uq[j], a_w_ukv[j])
                if not sample:
                    st[0].append(ckv.reshape(batch, seq, kv_lora))
                    st[1].append(kpe[:, :ROPE_A].reshape(batch, seq, ROPE_A))
                w_o = a_w_o[j]
            elif kind == 1:
                o, full = _diff_mixer(x, mod, g[0:1], groups, t, sample, sq, cache_b_k[:, j], cache_b_v[:, j],
                                      b_w_qkv[j], b_lambda[j], b_subln[j], _lambda_init(l))
                if not sample:
                    d_b = H_B * 2 * DH_B
                    st[2].append(full[:, d_b:2 * d_b].reshape(batch, seq, H_B, 2, DH_B))
                    st[3].append(full[:, 2 * d_b:].reshape(batch, seq, H_B, 2 * DH_B))
                w_o = b_w_o[j]
            else:
                o, full = _gqa_mixer(x, mod, g[0:1], groups, t, sample, sq, cache_c_k[:, j], cache_c_v[:, j],
                                     c_w_qkv[j], c_q_norm[j], c_k_norm[j])
                if not sample:
                    dq_all, dk_all = H_C * HD_C, KVH_C * HD_C
                    st[4].append(full[:, dq_all:dq_all + dk_all].reshape(batch, seq, KVH_C, HD_C))
                    st[5].append(full[:, dq_all + dk_all:].reshape(batch, seq, KVH_C, HD_C))
                w_o = c_w_o[j]
            x = _out_proj(o.reshape(x.shape[0], -1), w_o.astype(BF16), x, mod, g[1:2], groups, 2)
            x = _conv_ffn(x, mod, g[2:3], g[3:4], ffn_w_in[l].astype(BF16), ffn_conv_w[l], ffn_conv_b[l][None],
                          ffn_w_down[l].astype(BF16), groups, sq)
            new.append(x)
        xp, xs = new

    states = tuple(jnp.stack(s, axis=1) for s in st)
    return (xp.reshape(batch, seq, d), xs.reshape(dec_batch, dec_seq, d)) + states
```

```python
import functools
import math

import jax
import jax.numpy as jnp
import numpy as np
from jax import lax
from jax.experimental import pallas as pl
from jax.experimental.pallas import tpu as pltpu

F32 = jnp.float32
BF16 = jnp.bfloat16

EPS = 1e-6
LOG2E = math.log2(math.e)
ROPE_BASE = 10000.0
GRID_W = 64
N_MIXERS = 3
H_A = 16
NOPE_A = 128
ROPE_A = 64
V_A = 128
H_B = 16
DH_B = 64
H_C = 16
KVH_C = 4
HD_C = 128

LANES = 128
MXU_COLS = 256
VMEM_LIMIT = 56 * 1024 * 1024
ROW_TILE = 512
ATTN_ROWS = 1024
KV_CHUNK = 1536


def _params(*sem):
    return pltpu.CompilerParams(dimension_semantics=sem, vmem_limit_bytes=VMEM_LIMIT)


def _largest_tile(n, cap):
    for tile in range(cap, 0, -LANES):
        if n % tile == 0:
            return tile
    return n


def _dot(a, b):
    return jnp.dot(a, b, preferred_element_type=F32)


def _dot_nt(a, b):
    return lax.dot_general(a, b, (((1,), (1,)), ((), ())), preferred_element_type=F32)


def _rms(x, g):
    return x * lax.rsqrt(jnp.mean(x * x, axis=-1, keepdims=True) + EPS) * g


def _modnorm(x, g, shift, scale):
    return _rms(x, g) * (1.0 + scale) + shift


def _rope(x, cos, sin, half):
    w = x.shape[-1]
    lane = lax.broadcasted_iota(jnp.int32, x.shape, 1)
    first = (lane & (2 * half - 1)) < half
    partner = jnp.where(first, pltpu.roll(x, w - half, 1), pltpu.roll(x, half, 1))
    return x * cos + partner * sin


def _rope_base(t, dim):
    pos = jnp.arange(t)
    row = (pos // GRID_W).astype(F32)
    col = (pos % GRID_W).astype(F32)
    r = dim // 2
    half = r // 2
    freqs = ROPE_BASE ** (-jnp.arange(half, dtype=F32) / half)
    lane = np.arange(dim)
    idx = (lane % r) % half
    first = (lane % r) < half
    use_col = (lane // r) == 1
    ang = jnp.where(use_col[None, :], col[:, None], row[:, None]) * freqs[idx][None, :]
    return jnp.cos(ang), jnp.where(first[None, :], -jnp.sin(ang), jnp.sin(ang))


def _rope_tables(t, dim, left, right, reps=1, scale=1.0):
    cos, sin = _rope_base(t, dim)
    cos = jnp.concatenate([jnp.ones((t, left), F32)] + [cos] * reps + [jnp.ones((t, right), F32)], axis=1)
    sin = jnp.concatenate([jnp.zeros((t, left), F32)] + [sin] * reps + [jnp.zeros((t, right), F32)], axis=1)
    return cos * scale, sin * scale


def _mod_kernel(c_ref, w_ref, b_ref, o_ref):
    c = c_ref[...]
    s = c * (1.0 / (1.0 + jnp.exp(-c)))
    o_ref[0] = _dot(s.astype(BF16), w_ref[0].astype(BF16)) + b_ref[0]


def _modulation(cvec, w_mod, b_mod):
    depth, d, n = w_mod.shape
    tn = _largest_tile(n, 1024)
    return pl.pallas_call(
        _mod_kernel,
        out_shape=jax.ShapeDtypeStruct((depth, cvec.shape[0], n), F32),
        grid=(depth, n // tn),
        in_specs=[pl.BlockSpec(cvec.shape, lambda l, j: (0, 0)),
                  pl.BlockSpec((1, d, tn), lambda l, j: (l, 0, j)),
                  pl.BlockSpec((1, 1, tn), lambda l, j: (l, 0, j))],
        out_specs=pl.BlockSpec((1, cvec.shape[0], tn), lambda l, j: (l, 0, j)),
        compiler_params=_params("arbitrary", "arbitrary"),
        name="adaln_modulation",
    )(cvec, w_mod, b_mod.reshape(depth, 1, n))


def _tile_rows(t):
    return min(ROW_TILE, t)


def _mla_down_kernel(*refs, ql, kvl, rope):
    if rope:
        x_ref, mod_ref, g_ref, w_ref, qn_ref, kvn_ref, cos_ref, sin_ref, cq_ref, ckv_ref, kpe_ref = refs
    else:
        x_ref, mod_ref, g_ref, w_ref, qn_ref, kvn_ref, cq_ref, ckv_ref, kpe_ref = refs
    h = _modnorm(x_ref[...], g_ref[...], mod_ref[0, 0:1, :], mod_ref[0, 1:2, :])
    d = _dot(h.astype(BF16), w_ref[...])
    cq_ref[...] = _rms(d[:, :ql], qn_ref[...]).astype(BF16)
    ckv_ref[...] = _rms(d[:, ql:ql + kvl], kvn_ref[...])
    kpe = d[:, ql + kvl:]
    if rope:
        kpe = _rope(kpe, cos_ref[...], sin_ref[...], ROPE_A // 4)
    kpe_ref[...] = kpe


def _mla_down(x, mod, g, w, qn, kvn, tables, groups):
    m, d = x.shape
    t = m // groups
    tm = _tile_rows(t)
    nt = t // tm
    ql, kvl = qn.shape[1], kvn.shape[1]
    n = w.shape[1]
    rope = tables is not None
    row = lambda gi, i: (gi * nt + i, 0)
    const = lambda gi, i: (0, 0)
    in_specs = [pl.BlockSpec((tm, d), row),
                pl.BlockSpec((1, 6, d), lambda gi, i: (gi, 0, 0)),
                pl.BlockSpec((1, d), const),
                pl.BlockSpec((d, n), const),
                pl.BlockSpec((1, ql), const),
                pl.BlockSpec((1, kvl), const)]
    args = [x, mod, g, w, qn, kvn]
    if rope:
        in_specs += [pl.BlockSpec((tm, LANES), lambda gi, i: (i, 0))] * 2
        args += list(tables)
    return pl.pallas_call(
        functools.partial(_mla_down_kernel, ql=ql, kvl=kvl, rope=rope),
        out_shape=(jax.ShapeDtypeStruct((m, ql), BF16),
                   jax.ShapeDtypeStruct((m, kvl), F32),
                   jax.ShapeDtypeStruct((m, n - ql - kvl), F32)),
        grid=(groups, nt),
        in_specs=in_specs,
        out_specs=(pl.BlockSpec((tm, ql), row), pl.BlockSpec((tm, kvl), row),
                   pl.BlockSpec((tm, n - ql - kvl), row)),
        compiler_params=_params("arbitrary", "arbitrary"),
        name="mla_down_proj",
    )(*args)


def _qkv_kernel(*refs, rope, want_f32, n_q, n_k, pattern, half, qk_norm, q_scale):
    refs = list(refs)
    x_ref, mod_ref, g_ref, w_ref = refs[:4]
    pos = 4
    if qk_norm:
        qn_ref, kn_ref = refs[pos:pos + 2]
        pos += 2
    if rope:
        cos_ref, sin_ref = refs[pos:pos + 2]
        pos += 2
    ob_ref = refs[pos]
    pos += 1
    if want_f32:
        of_ref = refs[pos]
        pos += 1
    h_ref = refs[pos]
    j = pl.program_id(2)

    @pl.when(j == 0)
    def _():
        h_ref[...] = _modnorm(x_ref[...], g_ref[...], mod_ref[0, 0:1, :], mod_ref[0, 1:2, :]).astype(BF16)

    acc = _dot(h_ref[...], w_ref[...])
    tn = acc.shape[1]

    def emit(kind):
        for c in range(tn // pattern):
            sl = slice(c * pattern, (c + 1) * pattern)
            y = acc[:, sl]
            if kind != "v":
                if qk_norm:
                    y = _rms(y, qn_ref[...] if kind == "q" else kn_ref[...])
                if want_f32:
                    of_ref[:, sl] = y
                if rope:
                    y = _rope(y, cos_ref[...], sin_ref[...], half)
                if kind == "q":
                    y = y * q_scale
            elif want_f32:
                of_ref[:, sl] = y
            ob_ref[:, sl] = y.astype(BF16)

    pl.when(j < n_q)(lambda: emit("q"))
    pl.when((j >= n_q) & (j < n_q + n_k))(lambda: emit("k"))
    pl.when(j >= n_q + n_k)(lambda: emit("v"))


def _qkv_proj(x, mod, g, w, groups, *, tn, n_q, n_k, pattern, half, q_scale, tables=None, norms=None,
              want_f32=False):
    m, d = x.shape
    n = w.shape[1]
    t = m // groups
    tm = _tile_rows(t)
    nt = t // tm
    rope = tables is not None
    qk_norm = norms is not None
    row = lambda gi, i, j: (gi * nt + i, 0)
    const = lambda gi, i, j: (0, 0)
    tile = lambda gi, i, j: (gi * nt + i, j)
    in_specs = [pl.BlockSpec((tm, d), row),
                pl.BlockSpec((1, 6, d), lambda gi, i, j: (gi, 0, 0)),
                pl.BlockSpec((1, d), const),
                pl.BlockSpec((d, tn), lambda gi, i, j: (0, j))]
    args = [x, mod, g, w]
    if qk_norm:
        in_specs += [pl.BlockSpec((1, pattern), const)] * 2
        args += list(norms)
    if rope:
        in_specs += [pl.BlockSpec((tm, pattern), lambda gi, i, j: (i, 0))] * 2
        args += list(tables)
    out_shape = [jax.ShapeDtypeStruct((m, n), BF16)]
    out_specs = [pl.BlockSpec((tm, tn), tile)]
    if want_f32:
        out_shape.append(jax.ShapeDtypeStruct((m, n), F32))
        out_specs.append(pl.BlockSpec((tm, tn), tile))
    return pl.pallas_call(
        functools.partial(_qkv_kernel, rope=rope, want_f32=want_f32, n_q=n_q, n_k=n_k, pattern=pattern,
                          half=half, qk_norm=qk_norm, q_scale=q_scale),
        out_shape=tuple(out_shape),
        grid=(groups, nt, n // tn),
        in_specs=in_specs,
        out_specs=tuple(out_specs),
        scratch_shapes=[pltpu.VMEM((tm, d), BF16)],
        compiler_params=_params("arbitrary", "arbitrary", "arbitrary"),
        name="qkv_proj",
    )(*args)


def _mla_uq_kernel(*refs, rope, pattern, scale):
    if rope:
        x_ref, w_ref, cos_ref, sin_ref, o_ref = refs
    else:
        x_ref, w_ref, o_ref = refs
    acc = _dot(x_ref[...], w_ref[...])
    for c in range(acc.shape[1] // pattern):
        nope = slice(c * pattern, c * pattern + NOPE_A)
        pe = slice(c * pattern + NOPE_A, (c + 1) * pattern)
        o_ref[:, nope] = (acc[:, nope] * scale).astype(BF16)
        y = acc[:, pe]
        if rope:
            y = _rope(y, cos_ref[...], sin_ref[...], ROPE_A // 4)
        o_ref[:, pe] = (y * scale).astype(BF16)


def _mla_uq(cq, w, tables, t, scale):
    m, k = cq.shape
    n = w.shape[1]
    tm = _tile_rows(t)
    nt = t // tm
    tn = min(1024, n)
    pattern = 2 * LANES
    rope = tables is not None
    in_specs = [pl.BlockSpec((tm, k), lambda i, j: (i, 0)), pl.BlockSpec((k, tn), lambda i, j: (0, j))]
    args = [cq, w]
    if rope:
        in_specs += [pl.BlockSpec((tm, pattern - NOPE_A), lambda i, j: (i % nt, 0))] * 2
        args += list(tables)
    return pl.pallas_call(
        functools.partial(_mla_uq_kernel, rope=rope, pattern=pattern, scale=scale),
        out_shape=jax.ShapeDtypeStruct((m, n), BF16),
        grid=(m // tm, n // tn),
        in_specs=in_specs,
        out_specs=pl.BlockSpec((tm, tn), lambda i, j: (i, j)),
        compiler_params=_params("arbitrary", "arbitrary"),
        name="mla_q_up_proj",
    )(*args)


def _mla_ukv_kernel(x_ref, w_ref, kpe_ref, k_ref, v_ref, *, heads):
    acc = _dot(x_ref[...], w_ref[...])
    kpe = kpe_ref[...]
    for h in range(heads):
        k_ref[:, 2 * h * LANES:(2 * h + 1) * LANES] = acc[:, 2 * h * LANES:(2 * h + 1) * LANES].astype(BF16)
        k_ref[:, (2 * h + 1) * LANES:(2 * h + 2) * LANES] = kpe
        v_ref[:, h * LANES:(h + 1) * LANES] = acc[:, (2 * h + 1) * LANES:(2 * h + 2) * LANES].astype(BF16)


def _mla_ukv(ckv, w, kpe):
    m, k = ckv.shape
    n = w.shape[1]
    tm = 512 if m % 512 == 0 else m
    heads = min(4, H_A)
    tn = heads * 2 * LANES
    return pl.pallas_call(
        functools.partial(_mla_ukv_kernel, heads=heads),
        out_shape=(jax.ShapeDtypeStruct((m, n), BF16), jax.ShapeDtypeStruct((m, n // 2), BF16)),
        grid=(m // tm, n // tn),
        in_specs=[pl.BlockSpec((tm, k), lambda i, j: (i, 0)),
                  pl.BlockSpec((k, tn), lambda i, j: (0, j)),
                  pl.BlockSpec((tm, LANES), lambda i, j: (i, 0))],
        out_specs=(pl.BlockSpec((tm, tn), lambda i, j: (i, j)),
                   pl.BlockSpec((tm, tn // 2), lambda i, j: (i, j))),
        compiler_params=_params("arbitrary", "arbitrary"),
        name="mla_kv_up_proj",
    )(ckv, w, kpe)


def _attn_kernel(*refs, diff, heads, reps, dq, dv, tq, length, tk, lam_init):
    if diff:
        q_ref, k_ref, v_ref, lam_ref, subln_ref, o_ref, vt_ref = refs
    else:
        q_ref, k_ref, v_ref, o_ref, vt_ref = refs

    @pl.when(pl.program_id(2) == 0)
    def _():
        vt_ref[...] = v_ref[0].T

    if diff:
        lv = lam_ref[...]
        lam = (jnp.exp(jnp.sum(lv[0:1] * lv[1:2], axis=-1, keepdims=True))
               - jnp.exp(jnp.sum(lv[2:3] * lv[3:4], axis=-1, keepdims=True)) + lam_init)
    nc = length // tk

    for hh in range(heads):
        q = q_ref[0, :, hh * reps * dq:(hh + 1) * reps * dq]
        if diff:
            lane = lax.broadcasted_iota(jnp.int32, q.shape, 1)
            zero = jnp.zeros_like(q)
            qs = jnp.concatenate([jnp.where(lane < dq // 2, q, zero), jnp.where(lane >= dq // 2, q, zero)],
                                 axis=0)
        elif reps > 1:
            qs = jnp.concatenate([q[:, r * dq:(r + 1) * dq] for r in range(reps)], axis=0)
        else:
            qs = q

        def scores(c):
            return _dot_nt(k_ref[0, c * tk:(c + 1) * tk, hh * dq:(hh + 1) * dq], qs)

        m = l = acc = None
        st_next = scores(0)
        for c in range(nc):
            st, st_next = st_next, (scores(c + 1) if c + 1 < nc else None)
            cmax = jnp.max(st, axis=0, keepdims=True)
            m_new = cmax if c == 0 else jnp.maximum(m, cmax)
            p = jnp.exp2(st - m_new)
            psum = jnp.sum(p, axis=0, keepdims=True)
            pv = _dot(vt_ref[hh * dv:(hh + 1) * dv, c * tk:(c + 1) * tk], p.astype(BF16))
            if c == 0:
                l, acc = psum, pv
            else:
                alpha = jnp.exp2(m - m_new)
                l = alpha * l + psum
                acc = alpha * acc + pv
            m = m_new
        if diff:
            ot = acc[:, :tq] / l[:, :tq] - lam * (acc[:, tq:] / l[:, tq:])
            o_ref[0, :, hh * dv:(hh + 1) * dv] = (_rms(ot.T, subln_ref[...]) * (1.0 - lam_init)).astype(BF16)
        else:
            o = (acc / l).T
            for r in range(reps):
                o_ref[0, :, (hh * reps + r) * dv:(hh * reps + r + 1) * dv] = o[r * tq:(r + 1) * tq].astype(BF16)


def _attention(q, k, v, *, kv_heads, reps, dq, dv, diff=False, lam=None, subln=None, lam_init=0.0,
               q_off=0, k_off=0, v_off=0):
    b, s, _ = q.shape
    length = k.shape[1]
    stack = 2 if diff else reps
    tq = min(ATTN_ROWS // stack, s)
    tk = min(KV_CHUNK, length)
    heads = kv_heads if s == tq and length == tk else 1
    assert q_off % heads == 0 and k_off % heads == 0 and v_off % heads == 0
    in_specs = [pl.BlockSpec((1, tq, heads * reps * dq), lambda bi, h, i: (bi, i, q_off // heads + h)),
                pl.BlockSpec((1, length, heads * dq), lambda bi, h, i: (bi, 0, k_off // heads + h)),
                pl.BlockSpec((1, length, heads * dv), lambda bi, h, i: (bi, 0, v_off // heads + h))]
    args = [q, k, v]
    if diff:
        in_specs += [pl.BlockSpec(lam.shape, lambda bi, h, i: (0, 0)),
                     pl.BlockSpec(subln.shape, lambda bi, h, i: (0, 0))]
        args += [lam, subln]
    return pl.pallas_call(
        functools.partial(_attn_kernel, diff=diff, heads=heads, reps=reps, dq=dq, dv=dv, tq=tq, length=length,
                          tk=tk, lam_init=lam_init),
        out_shape=jax.ShapeDtypeStruct((b, s, kv_heads * reps * dv), BF16),
        grid=(b, kv_heads // heads, s // tq),
        in_specs=in_specs,
        out_specs=pl.BlockSpec((1, tq, heads * reps * dv), lambda bi, h, i: (bi, i, h)),
        scratch_shapes=[pltpu.VMEM((heads * dv, length), BF16)],
        compiler_params=_params("arbitrary", "arbitrary", "arbitrary"),
        name="diff_attention" if diff else "attention",
    )(*args)


def _out_proj_kernel(o_ref, w_ref, x_ref, mod_ref, g_ref, y_ref, *, gate_row):
    f = _dot(o_ref[...], w_ref[...])
    y_ref[...] = x_ref[...] + mod_ref[0, gate_row:gate_row + 1, :] * _rms(f, g_ref[...])


def _out_proj(o, w, x, mod, g, groups, gate_row):
    m, d = x.shape
    k = o.shape[1]
    t = m // groups
    tm = _tile_rows(t)
    nt = t // tm
    row = lambda gi, i: (gi * nt + i, 0)
    const = lambda gi, i: (0, 0)
    return pl.pallas_call(
        functools.partial(_out_proj_kernel, gate_row=gate_row),
        out_shape=jax.ShapeDtypeStruct((m, d), F32),
        grid=(groups, nt),
        in_specs=[pl.BlockSpec((tm, k), row),
                  pl.BlockSpec((k, d), const),
                  pl.BlockSpec((tm, d), row),
                  pl.BlockSpec((1, 6, d), lambda gi, i: (gi, 0, 0)),
                  pl.BlockSpec((1, d), const)],
        out_specs=pl.BlockSpec((tm, d), row),
        compiler_params=_params("arbitrary", "arbitrary"),
        name="out_proj_residual",
    )(o, w, x, mod, g)


HALO = 8


def _ffn_kernel(x_ref, xp_ref, xn_ref, mod_ref, g_ref, go_ref, wg_ref, wv_ref, cwg_ref, cwv_ref, cbg_ref,
                cbv_ref, wd_ref, y_ref, h_ref, act_ref, acc_ref, *, tm, tf, seq, nf):
    i = pl.program_id(1)
    j = pl.program_id(2)
    halo_only = seq % tm == 0

    @pl.when(j == 0)
    def _():
        shift, scale = mod_ref[0, 3:4, :], mod_ref[0, 4:5, :]
        h_ref[0:tm, :] = _modnorm(x_ref[...], g_ref[...], shift, scale).astype(BF16)
        halo = _modnorm(jnp.concatenate([xp_ref[...], xn_ref[...]], axis=0), g_ref[...], shift, scale)
        if halo_only:
            first = ((i * tm) % seq == 0).astype(F32)
            last = (((i + 1) * tm) % seq == 0).astype(F32)
            ones = jnp.ones((HALO, 1), F32)
            halo = halo * jnp.concatenate([ones - first, ones - last], axis=0)
        h_ref[tm:tm + 2 * HALO, :] = halo.astype(BF16)
        acc_ref[...] = jnp.zeros_like(acc_ref)

    if not halo_only:
        tok = i * tm + lax.broadcasted_iota(jnp.int32, (tm, 1), 0)
        seq_first = (tok % seq) == 0
        seq_last = (tok % seq) == seq - 1

    def conv(u, cw_ref, cb_ref, cols):
        ext = jnp.concatenate([u[tm:tm + HALO], u[0:tm], u[tm + HALO:tm + 2 * HALO]], axis=0)
        um = u[0:tm]
        up = pltpu.roll(ext, 1, 0)[HALO:HALO + tm]
        un = pltpu.roll(ext, tm + 2 * HALO - 1, 0)[HALO:HALO + tm]
        if not halo_only:
            up = jnp.where(seq_first, 0.0, up)
            un = jnp.where(seq_last, 0.0, un)
        return (up * cw_ref[0, 0:1, cols] + um * cw_ref[0, 1:2, cols] + un * cw_ref[0, 2:3, cols]
                + cb_ref[0, :, cols])

    def step(up, down, last, slot):
        def up_piece(c):
            cols = slice(c * MXU_COLS, (c + 1) * MXU_COLS)
            return _dot(h_ref[...], wg_ref[0, :, cols]), _dot(h_ref[...], wv_ref[0, :, cols])

        def gate_piece(c, ug, uv):
            cols = slice(c * MXU_COLS, (c + 1) * MXU_COLS)
            gate = conv(ug, cwg_ref, cbg_ref, cols)
            val = conv(uv, cwv_ref, cbv_ref, cols)
            act_ref[slot, :, cols] = (gate * (1.0 / (1.0 + jnp.exp(-gate))) * val).astype(BF16)

        def down_piece(c):
            cols = slice(c * MXU_COLS, (c + 1) * MXU_COLS)
            acc_ref[:, cols] += _dot(act_ref[1 - slot], wd_ref[0, :, cols])

        n_up = tf // MXU_COLS if up else 0
        n_down = acc_ref.shape[1] // MXU_COLS if down else 0
        pending = None
        for c in range(n_up):
            u = up_piece(c)
            if pending is not None:
                gate_piece(c - 1, *pending)
            pending = u
        for c in range(n_down):
            if c == n_down // 2 and pending is not None:
                gate_piece(n_up - 1, *pending)
                pending = None
            down_piece(c)
        if pending is not None:
            gate_piece(n_up - 1, *pending)
        if last:
            y_ref[...] = x_ref[...] + mod_ref[0, 5:6, :] * _rms(acc_ref[...], go_ref[...])

    for up, down, last in sorted({(s < nf, s >= 1, s == nf) for s in range(nf + 1)}):
        cond = ((j < nf) == up) & ((j >= 1) == down) & ((j == nf) == last)
        for parity in range(2):
            pl.when(cond & (j % 2 == parity))(functools.partial(step, up, down, last, parity))


FF_CHUNK = 512


def _ffn_chunks(ff):
    tf = FF_CHUNK if ff % FF_CHUNK == 0 else ff
    return tf, ff // tf


def _ffn_prepare_w_in(w_in):
    depth, d, ff2 = w_in.shape
    tf, _ = _ffn_chunks(ff2 // 2)
    return w_in.astype(BF16).reshape(depth, d, ff2 // tf, tf).transpose(0, 2, 1, 3).reshape(-1, d, tf)


def _conv_ffn(x, mod, g_in, g_out, w_in, conv_w, conv_b, w_down, layer, groups, seq):
    m, d = x.shape
    ff = w_down.shape[1]
    t = m // groups
    tm = _tile_rows(t)
    nt = t // tm
    tf, nf = _ffn_chunks(ff)
    hb = tm // HALO
    last_halo = m // HALO - 1
    row = lambda gi, i, j: (gi * nt + i, 0)
    const = lambda gi, i, j: (0, 0)
    up = lambda j: jnp.minimum(j, nf - 1)
    down = lambda j: jnp.maximum(j - 1, 0)
    return pl.pallas_call(
        functools.partial(_ffn_kernel, tm=tm, tf=tf, seq=seq, nf=nf),
        out_shape=jax.ShapeDtypeStruct((m, d), F32),
        grid=(groups, nt, nf + 1),
        in_specs=[pl.BlockSpec((tm, d), row),
                  pl.BlockSpec((HALO, d), lambda gi, i, j: (jnp.maximum((gi * nt + i) * hb - 1, 0), 0)),
                  pl.BlockSpec((HALO, d), lambda gi, i, j: (jnp.minimum((gi * nt + i + 1) * hb, last_halo), 0)),
                  pl.BlockSpec((1, 6, d), lambda gi, i, j: (gi, 0, 0)),
                  pl.BlockSpec((1, d), const),
                  pl.BlockSpec((1, d), const),
                  pl.BlockSpec((1, d, tf), lambda gi, i, j: (layer * 2 * nf + up(j), 0, 0)),
                  pl.BlockSpec((1, d, tf), lambda gi, i, j: (layer * 2 * nf + nf + up(j), 0, 0)),
                  pl.BlockSpec((1, 3, tf), lambda gi, i, j: (layer, 0, up(j))),
                  pl.BlockSpec((1, 3, tf), lambda gi, i, j: (layer, 0, nf + up(j))),
                  pl.BlockSpec((1, 1, tf), lambda gi, i, j: (layer, 0, up(j))),
                  pl.BlockSpec((1, 1, tf), lambda gi, i, j: (layer, 0, nf + up(j))),
                  pl.BlockSpec((1, tf, d), lambda gi, i, j: (layer, down(j), 0))],
        out_specs=pl.BlockSpec((tm, d), row),
        scratch_shapes=[pltpu.VMEM((tm + 2 * HALO, d), BF16), pltpu.VMEM((2, tm, tf), BF16),
                        pltpu.VMEM((tm, d), F32)],
        compiler_params=_params("arbitrary", "arbitrary", "arbitrary"),
        name="conv_ffn",
    )(x, x, x, mod, g_in, g_out, w_in, w_in, conv_w, conv_w, conv_b, conv_b, w_down)


def _pad_lanes(a, width):
    return jnp.pad(a, [(0, 0)] * (a.ndim - 1) + [(0, width - a.shape[-1])])


def _mla_mixer(x, mod, g, groups, t, sample, seq, cache_ckv, cache_kpe, w_down, q_norm, kv_norm, w_uq, w_ukv):
    m = x.shape[0]
    ql, kvl = q_norm.shape[0], kv_norm.shape[0]
    scale = LOG2E / math.sqrt(NOPE_A + ROPE_A)
    kpe_tables = _rope_tables(t, ROPE_A, 0, LANES - ROPE_A) if sample else None
    q_tables = kpe_tables
    w_down_p = _pad_lanes(w_down, ql + kvl + LANES).astype(BF16)
    cq, ckv, kpe = _mla_down(x, mod, g, w_down_p, q_norm[None], kv_norm[None], kpe_tables, groups)
    w_uq_p = _pad_lanes(w_uq.reshape(ql, H_A, NOPE_A + ROPE_A), 2 * LANES).reshape(ql, H_A * 2 * LANES)
    q = _mla_uq(cq, w_uq_p.astype(BF16), q_tables, t, scale)
    if sample:
        b = groups
        ckv_all = jnp.concatenate([cache_ckv.astype(BF16), ckv.astype(BF16).reshape(b, t, kvl)], axis=1)
        kpe_all = jnp.concatenate([_pad_lanes(cache_kpe, LANES).astype(BF16),
                                   kpe.astype(BF16).reshape(b, t, LANES)], axis=1)
        length = ckv_all.shape[1]
        s = t
    else:
        s = seq
        b = m // s
        ckv_all, kpe_all, length = ckv.astype(BF16), kpe.astype(BF16), s
    k, v = _mla_ukv(ckv_all.reshape(b * length, kvl), w_ukv.astype(BF16), kpe_all.reshape(b * length, LANES))
    o = _attention(q.reshape(b, s, H_A * 2 * LANES), k.reshape(b, length, H_A * 2 * LANES),
                   v.reshape(b, length, H_A * V_A), kv_heads=H_A, reps=1, dq=2 * LANES, dv=V_A)
    return o, ckv, kpe


def _diff_mixer(x, mod, g, groups, t, sample, seq, cache_k, cache_v, w_qkv, lam_vecs, subln, lam_init):
    m = x.shape[0]
    d_b = H_B * 2 * DH_B
    tables = _rope_tables(t, DH_B, 0, 0, reps=2) if sample else None
    outs = _qkv_proj(x, mod, g, w_qkv.astype(BF16), groups, tn=min(1024, d_b), n_q=d_b // min(1024, d_b),
                     n_k=d_b // min(1024, d_b), pattern=2 * DH_B, half=DH_B // 4,
                     q_scale=LOG2E / math.sqrt(DH_B), tables=tables, want_f32=not sample)
    qkv = outs[0]
    hb = d_b // (2 * DH_B)
    if sample:
        b = groups
        qkv = qkv.reshape(b, t, 3 * d_b)
        past = cache_k.shape[1]
        k = jnp.concatenate([cache_k.reshape(b, past, d_b).astype(BF16), qkv[:, :, d_b:2 * d_b]], axis=1)
        v = jnp.concatenate([cache_v.reshape(b, past, d_b).astype(BF16), qkv[:, :, 2 * d_b:]], axis=1)
        o = _attention(qkv, k, v, kv_heads=H_B, reps=1, dq=2 * DH_B, dv=2 * DH_B, diff=True, lam=lam_vecs,
                       subln=subln[None], lam_init=lam_init)
        return o, None
    b = m // seq
    qkv = qkv.reshape(b, seq, 3 * d_b)
    o = _attention(qkv, qkv, qkv, kv_heads=H_B, reps=1, dq=2 * DH_B, dv=2 * DH_B, diff=True, lam=lam_vecs,
                   subln=subln[None], lam_init=lam_init, k_off=hb, v_off=2 * hb)
    return o, outs[1]


def _gqa_mixer(x, mod, g, groups, t, sample, seq, cache_k, cache_v, w_qkv, q_norm, k_norm):
    m = x.shape[0]
    dq_all, dk_all = H_C * HD_C, KVH_C * HD_C
    tn = dk_all
    tables = _rope_tables(t, HD_C, 0, 0) if sample else None
    outs = _qkv_proj(x, mod, g, w_qkv.astype(BF16), groups, tn=tn, n_q=dq_all // tn, n_k=1, pattern=HD_C,
                     half=HD_C // 4, q_scale=LOG2E / math.sqrt(HD_C), tables=tables,
                     norms=(q_norm[None], k_norm[None]), want_f32=not sample)
    qkv = outs[0]
    reps = H_C // KVH_C
    if sample:
        b = groups
        qkv = qkv.reshape(b, t, dq_all + 2 * dk_all)
        past = cache_k.shape[1]
        k = jnp.concatenate([cache_k.reshape(b, past, dk_all).astype(BF16),
                             qkv[:, :, dq_all:dq_all + dk_all]], axis=1)
        v = jnp.concatenate([cache_v.reshape(b, past, dk_all).astype(BF16), qkv[:, :, dq_all + dk_all:]], axis=1)
        o = _attention(qkv, k, v, kv_heads=KVH_C, reps=reps, dq=HD_C, dv=HD_C)
        return o, None
    b = m // seq
    qkv = qkv.reshape(b, seq, dq_all + 2 * dk_all)
    o = _attention(qkv, qkv, qkv, kv_heads=KVH_C, reps=reps, dq=HD_C, dv=HD_C, k_off=H_C, v_off=H_C + KVH_C)
    return o, outs[1]


def _lambda_init(layer):
    return 0.8 - 0.6 * math.exp(-0.3 * layer)


def kernel(x_prompt, x_sample, c, cache_a_ckv, cache_a_kpe, cache_b_k, cache_b_v, cache_c_k, cache_c_v, c_ctx, norm_g, w_mod, b_mod, ffn_w_in, ffn_conv_w, ffn_conv_b, ffn_w_down, a_w_down, a_q_norm, a_kv_norm, a_w_uq, a_w_ukv, a_w_o, b_w_qkv, b_lambda, b_subln, b_w_o, c_w_qkv, c_q_norm, c_k_norm, c_w_o):
    batch, seq, d = x_prompt.shape
    dec_batch, dec_seq, _ = x_sample.shape
    depth = w_mod.shape[0]
    tp = batch * seq

    cvec = jnp.concatenate([c, c_ctx[None], jnp.zeros((8 - dec_batch - 1, d), F32)], axis=0)
    mods = _modulation(cvec, w_mod, b_mod)

    xp = x_prompt.reshape(tp, d)
    xs = x_sample.reshape(dec_batch * dec_seq, d)
    kv_lora = a_kv_norm.shape[1]
    st = [[] for _ in range(6)]
    w_in_all = _ffn_prepare_w_in(ffn_w_in)
    w_down_all = ffn_w_down.astype(BF16)
    conv_b_all = ffn_conv_b[:, None, :]

    for l in range(depth):
        kind, j = l % N_MIXERS, l // N_MIXERS
        mod_s = mods[l, :dec_batch].reshape(dec_batch, 6, d)
        mod_p = mods[l, dec_batch:dec_batch + 1].reshape(1, 6, d)
        g = norm_g[l]
        streams = ((xp, mod_p, 1, tp, False, seq), (xs, mod_s, dec_batch, dec_seq, True, dec_seq))
        new = []
        for x, mod, groups, t, sample, sq in streams:
            if kind == 0:
                o, ckv, kpe = _mla_mixer(x, mod, g[0:1], groups, t, sample, sq, cache_a_ckv[:, j], cache_a_kpe[:, j],
                                         a_w_down[j], a_q_norm[j], a_kv_norm[j], a_w_uq[j], a_w_ukv[j])
                if not sample:
                    st[0].append(ckv.reshape(batch, seq, kv_lora))
                    st[1].append(kpe[:, :ROPE_A].reshape(batch, seq, ROPE_A))
                w_o = a_w_o[j]
            elif kind == 1:
                o, full = _diff_mixer(x, mod, g[0:1], groups, t, sample, sq, cache_b_k[:, j], cache_b_v[:, j],
                                      b_w_qkv[j], b_lambda[j], b_subln[j], _lambda_init(l))
                if not sample:
                    d_b = H_B * 2 * DH_B
                    st[2].append(full[:, d_b:2 * d_b].reshape(batch, seq, H_B, 2, DH_B))
                    st[3].append(full[:, 2 * d_b:].reshape(batch, seq, H_B, 2 * DH_B))
                w_o = b_w_o[j]
            else:
                o, full = _gqa_mixer(x, mod, g[0:1], groups, t, sample, sq, cache_c_k[:, j], cache_c_v[:, j],
                                     c_w_qkv[j], c_q_norm[j], c_k_norm[j])
                if not sample:
                    dq_all, dk_all = H_C * HD_C, KVH_C * HD_C
                    st[4].append(full[:, dq_all:dq_all + dk_all].reshape(batch, seq, KVH_C, HD_C))
                    st[5].append(full[:, dq_all + dk_all:].reshape(batch, seq, KVH_C, HD_C))
                w_o = c_w_o[j]
            x = _out_proj(o.reshape(x.shape[0], -1), w_o.astype(BF16), x, mod, g[1:2], groups, 2)
            x = _conv_ffn(x, mod, g[2:3], g[3:4], w_in_all, ffn_conv_w, conv_b_all, w_down_all, l, groups, sq)
            new.append(x)
        xp, xs = new

    states = tuple(jnp.stack(s, axis=1) for s in st)
    return (xp.reshape(batch, seq, d), xs.reshape(dec_batch, dec_seq, d)) + states
```

```python
import functools
import math

import jax
import jax.numpy as jnp
import numpy as np
from jax import lax
from jax.experimental import pallas as pl
from jax.experimental.pallas import tpu as pltpu

F32 = jnp.float32
BF16 = jnp.bfloat16

EPS = 1e-6
LOG2E = math.log2(math.e)
ROPE_BASE = 10000.0
GRID_W = 64
N_MIXERS = 3
H_A = 16
NOPE_A = 128
ROPE_A = 64
V_A = 128
H_B = 16
DH_B = 64
H_C = 16
KVH_C = 4
HD_C = 128

LANES = 128
MXU_COLS = 256
VMEM_LIMIT = 56 * 1024 * 1024
ROW_TILE = 512
ATTN_ROWS = 1024
KV_CHUNK = 1536


def _params(*sem):
    return pltpu.CompilerParams(dimension_semantics=sem, vmem_limit_bytes=VMEM_LIMIT)


def _largest_tile(n, cap):
    for tile in range(cap, 0, -LANES):
        if n % tile == 0:
            return tile
    return n


def _dot(a, b):
    return jnp.dot(a, b, preferred_element_type=F32)


def _dot_nt(a, b):
    return lax.dot_general(a, b, (((1,), (1,)), ((), ())), preferred_element_type=F32)


def _rms(x, g):
    return x * lax.rsqrt(jnp.mean(x * x, axis=-1, keepdims=True) + EPS) * g


def _modnorm(x, g, shift, scale):
    return _rms(x, g) * (1.0 + scale) + shift


def _rope(x, cos, sin, half):
    w = x.shape[-1]
    lane = lax.broadcasted_iota(jnp.int32, x.shape, 1)
    first = (lane & (2 * half - 1)) < half
    partner = jnp.where(first, pltpu.roll(x, w - half, 1), pltpu.roll(x, half, 1))
    return x * cos + partner * sin


def _rope_base(t, dim):
    pos = jnp.arange(t)
    row = (pos // GRID_W).astype(F32)
    col = (pos % GRID_W).astype(F32)
    r = dim // 2
    half = r // 2
    freqs = ROPE_BASE ** (-jnp.arange(half, dtype=F32) / half)
    lane = np.arange(dim)
    idx = (lane % r) % half
    first = (lane % r) < half
    use_col = (lane // r) == 1
    ang = jnp.where(use_col[None, :], col[:, None], row[:, None]) * freqs[idx][None, :]
    return jnp.cos(ang), jnp.where(first[None, :], -jnp.sin(ang), jnp.sin(ang))


def _rope_tables(t, dim, left, right, reps=1, scale=1.0):
    cos, sin = _rope_base(t, dim)
    cos = jnp.concatenate([jnp.ones((t, left), F32)] + [cos] * reps + [jnp.ones((t, right), F32)], axis=1)
    sin = jnp.concatenate([jnp.zeros((t, left), F32)] + [sin] * reps + [jnp.zeros((t, right), F32)], axis=1)
    return cos * scale, sin * scale


def _mod_kernel(c_ref, w_ref, b_ref, o_ref):
    c = c_ref[...]
    s = c * (1.0 / (1.0 + jnp.exp(-c)))
    o_ref[0] = _dot(s.astype(BF16), w_ref[0].astype(BF16)) + b_ref[0]


def _modulation(cvec, w_mod, b_mod):
    depth, d, n = w_mod.shape
    tn = _largest_tile(n, 1024)
    return pl.pallas_call(
        _mod_kernel,
        out_shape=jax.ShapeDtypeStruct((depth, cvec.shape[0], n), F32),
        grid=(depth, n // tn),
        in_specs=[pl.BlockSpec(cvec.shape, lambda l, j: (0, 0)),
                  pl.BlockSpec((1, d, tn), lambda l, j: (l, 0, j)),
                  pl.BlockSpec((1, 1, tn), lambda l, j: (l, 0, j))],
        out_specs=pl.BlockSpec((1, cvec.shape[0], tn), lambda l, j: (l, 0, j)),
        compiler_params=_params("arbitrary", "arbitrary"),
        name="adaln_modulation",
    )(cvec, w_mod, b_mod.reshape(depth, 1, n))


def _tile_rows(t):
    return min(ROW_TILE, t)


def _mla_down_kernel(*refs, ql, kvl, rope):
    if rope:
        x_ref, mod_ref, g_ref, w_ref, qn_ref, kvn_ref, cos_ref, sin_ref, cq_ref, ckv_ref, kpe_ref = refs
    else:
        x_ref, mod_ref, g_ref, w_ref, qn_ref, kvn_ref, cq_ref, ckv_ref, kpe_ref = refs
    h = _modnorm(x_ref[...], g_ref[...], mod_ref[0, 0:1, :], mod_ref[0, 1:2, :])
    d = _dot(h.astype(BF16), w_ref[...])
    cq_ref[...] = _rms(d[:, :ql], qn_ref[...]).astype(BF16)
    ckv_ref[...] = _rms(d[:, ql:ql + kvl], kvn_ref[...])
    kpe = d[:, ql + kvl:]
    if rope:
        kpe = _rope(kpe, cos_ref[...], sin_ref[...], ROPE_A // 4)
    kpe_ref[...] = kpe


def _mla_down(x, mod, g, w, qn, kvn, tables, groups):
    m, d = x.shape
    t = m // groups
    tm = _tile_rows(t)
    nt = t // tm
    ql, kvl = qn.shape[1], kvn.shape[1]
    n = w.shape[1]
    rope = tables is not None
    row = lambda gi, i: (gi * nt + i, 0)
    const = lambda gi, i: (0, 0)
    in_specs = [pl.BlockSpec((tm, d), row),
                pl.BlockSpec((1, 6, d), lambda gi, i: (gi, 0, 0)),
                pl.BlockSpec((1, d), const),
                pl.BlockSpec((d, n), const),
                pl.BlockSpec((1, ql), const),
                pl.BlockSpec((1, kvl), const)]
    args = [x, mod, g, w, qn, kvn]
    if rope:
        in_specs += [pl.BlockSpec((tm, LANES), lambda gi, i: (i, 0))] * 2
        args += list(tables)
    return pl.pallas_call(
        functools.partial(_mla_down_kernel, ql=ql, kvl=kvl, rope=rope),
        out_shape=(jax.ShapeDtypeStruct((m, ql), BF16),
                   jax.ShapeDtypeStruct((m, kvl), F32),
                   jax.ShapeDtypeStruct((m, n - ql - kvl), F32)),
        grid=(groups, nt),
        in_specs=in_specs,
        out_specs=(pl.BlockSpec((tm, ql), row), pl.BlockSpec((tm, kvl), row),
                   pl.BlockSpec((tm, n - ql - kvl), row)),
        compiler_params=_params("arbitrary", "arbitrary"),
        name="mla_down_proj",
    )(*args)


def _qkv_kernel(*refs, rope, want_f32, n_q, n_k, n_tiles, pattern, half, qk_norm, q_scale):
    refs = list(refs)
    x_ref, mod_ref, g_ref, w_ref = refs[:4]
    pos = 4
    if qk_norm:
        qn_ref, kn_ref = refs[pos:pos + 2]
        pos += 2
    if rope:
        cos_ref, sin_ref = refs[pos:pos + 2]
        pos += 2
    ob_ref = refs[pos]
    pos += 1
    if want_f32:
        of_ref = refs[pos]
        pos += 1
    h_ref, acc_ref = refs[pos:pos + 2]
    j = pl.program_id(2)
    tn = w_ref.shape[1]

    def emit(kind, slot):
        for c in range(tn // pattern):
            sl = slice(c * pattern, (c + 1) * pattern)
            y = acc_ref[slot, :, sl]
            if kind != "v":
                if qk_norm:
                    y = _rms(y, qn_ref[...] if kind == "q" else kn_ref[...])
                if want_f32:
                    of_ref[:, sl] = y
                if rope:
                    y = _rope(y, cos_ref[...], sin_ref[...], half)
                if kind == "q":
                    y = y * q_scale
            elif want_f32:
                of_ref[:, sl] = y
            ob_ref[:, sl] = y.astype(BF16)

    def step(s):
        if s == 0:
            h_ref[...] = _modnorm(x_ref[...], g_ref[...], mod_ref[0, 0:1, :], mod_ref[0, 1:2, :]).astype(BF16)
        if s < n_tiles:
            acc_ref[s % 2] = _dot(h_ref[...], w_ref[...])
        if s >= 1:
            done = s - 1
            emit("q" if done < n_q else "k" if done < n_q + n_k else "v", done % 2)

    for s in range(n_tiles + 1):
        pl.when(j == s)(functools.partial(step, s))


def _qkv_proj(x, mod, g, w, groups, *, tn, n_q, n_k, pattern, half, q_scale, tables=None, norms=None,
              want_f32=False):
    m, d = x.shape
    n = w.shape[1]
    t = m // groups
    tm = _tile_rows(t)
    nt = t // tm
    rope = tables is not None
    qk_norm = norms is not None
    n_tiles = n // tn
    row = lambda gi, i, j: (gi * nt + i, 0)
    const = lambda gi, i, j: (0, 0)
    tile = lambda gi, i, j: (gi * nt + i, jnp.maximum(j - 1, 0))
    in_specs = [pl.BlockSpec((tm, d), row),
                pl.BlockSpec((1, 6, d), lambda gi, i, j: (gi, 0, 0)),
                pl.BlockSpec((1, d), const),
                pl.BlockSpec((d, tn), lambda gi, i, j: (0, jnp.minimum(j, n_tiles - 1)))]
    args = [x, mod, g, w]
    if qk_norm:
        in_specs += [pl.BlockSpec((1, pattern), const)] * 2
        args += list(norms)
    if rope:
        in_specs += [pl.BlockSpec((tm, pattern), lambda gi, i, j: (i, 0))] * 2
        args += list(tables)
    out_shape = [jax.ShapeDtypeStruct((m, n), BF16)]
    out_specs = [pl.BlockSpec((tm, tn), tile)]
    if want_f32:
        out_shape.append(jax.ShapeDtypeStruct((m, n), F32))
        out_specs.append(pl.BlockSpec((tm, tn), tile))
    return pl.pallas_call(
        functools.partial(_qkv_kernel, rope=rope, want_f32=want_f32, n_q=n_q, n_k=n_k, n_tiles=n_tiles,
                          pattern=pattern, half=half, qk_norm=qk_norm, q_scale=q_scale),
        out_shape=tuple(out_shape),
        grid=(groups, nt, n_tiles + 1),
        in_specs=in_specs,
        out_specs=tuple(out_specs),
        scratch_shapes=[pltpu.VMEM((tm, d), BF16), pltpu.VMEM((2, tm, tn), F32)],
        compiler_params=_params("arbitrary", "arbitrary", "arbitrary"),
        name="qkv_proj",
    )(*args)


def _mla_uq_kernel(*refs, rope, pattern, scale):
    if rope:
        x_ref, w_ref, cos_ref, sin_ref, o_ref = refs
    else:
        x_ref, w_ref, o_ref = refs
    acc = _dot(x_ref[...], w_ref[...])
    for c in range(acc.shape[1] // pattern):
        nope = slice(c * pattern, c * pattern + NOPE_A)
        pe = slice(c * pattern + NOPE_A, (c + 1) * pattern)
        o_ref[:, nope] = (acc[:, nope] * scale).astype(BF16)
        y = acc[:, pe]
        if rope:
            y = _rope(y, cos_ref[...], sin_ref[...], ROPE_A // 4)
        o_ref[:, pe] = (y * scale).astype(BF16)


def _mla_uq(cq, w, tables, t, scale):
    m, k = cq.shape
    n = w.shape[1]
    tm = _tile_rows(t)
    nt = t // tm
    tn = min(1024, n)
    pattern = 2 * LANES
    rope = tables is not None
    in_specs = [pl.BlockSpec((tm, k), lambda i, j: (i, 0)), pl.BlockSpec((k, tn), lambda i, j: (0, j))]
    args = [cq, w]
    if rope:
        in_specs += [pl.BlockSpec((tm, pattern - NOPE_A), lambda i, j: (i % nt, 0))] * 2
        args += list(tables)
    return pl.pallas_call(
        functools.partial(_mla_uq_kernel, rope=rope, pattern=pattern, scale=scale),
        out_shape=jax.ShapeDtypeStruct((m, n), BF16),
        grid=(m // tm, n // tn),
        in_specs=in_specs,
        out_specs=pl.BlockSpec((tm, tn), lambda i, j: (i, j)),
        compiler_params=_params("arbitrary", "arbitrary"),
        name="mla_q_up_proj",
    )(*args)


def _mla_ukv_kernel(x_ref, w_ref, kpe_ref, k_ref, v_ref, *, heads):
    acc = _dot(x_ref[...], w_ref[...])
    kpe = kpe_ref[...]
    for h in range(heads):
        k_ref[:, 2 * h * LANES:(2 * h + 1) * LANES] = acc[:, 2 * h * LANES:(2 * h + 1) * LANES].astype(BF16)
        k_ref[:, (2 * h + 1) * LANES:(2 * h + 2) * LANES] = kpe
        v_ref[:, h * LANES:(h + 1) * LANES] = acc[:, (2 * h + 1) * LANES:(2 * h + 2) * LANES].astype(BF16)


def _mla_ukv(ckv, w, kpe):
    m, k = ckv.shape
    n = w.shape[1]
    tm = 512 if m % 512 == 0 else m
    heads = min(4, H_A)
    tn = heads * 2 * LANES
    return pl.pallas_call(
        functools.partial(_mla_ukv_kernel, heads=heads),
        out_shape=(jax.ShapeDtypeStruct((m, n), BF16), jax.ShapeDtypeStruct((m, n // 2), BF16)),
        grid=(m // tm, n // tn),
        in_specs=[pl.BlockSpec((tm, k), lambda i, j: (i, 0)),
                  pl.BlockSpec((k, tn), lambda i, j: (0, j)),
                  pl.BlockSpec((tm, LANES), lambda i, j: (i, 0))],
        out_specs=(pl.BlockSpec((tm, tn), lambda i, j: (i, j)),
                   pl.BlockSpec((tm, tn // 2), lambda i, j: (i, j))),
        compiler_params=_params("arbitrary", "arbitrary"),
        name="mla_kv_up_proj",
    )(ckv, w, kpe)


def _attn_kernel(*refs, diff, heads, reps, dq, dv, tq, length, tk, lam_init):
    if diff:
        q_ref, k_ref, v_ref, lam_ref, subln_ref, o_ref, vt_ref = refs
    else:
        q_ref, k_ref, v_ref, o_ref, vt_ref = refs

    @pl.when(pl.program_id(2) == 0)
    def _():
        vt_ref[...] = v_ref[0].T

    if diff:
        lv = lam_ref[...]
        lam = (jnp.exp(jnp.sum(lv[0:1] * lv[1:2], axis=-1, keepdims=True))
               - jnp.exp(jnp.sum(lv[2:3] * lv[3:4], axis=-1, keepdims=True)) + lam_init)
    nc = length // tk

    for hh in range(heads):
        q = q_ref[0, :, hh * reps * dq:(hh + 1) * reps * dq]
        if diff:
            lane = lax.broadcasted_iota(jnp.int32, q.shape, 1)
            zero = jnp.zeros_like(q)
            qs = jnp.concatenate([jnp.where(lane < dq // 2, q, zero), jnp.where(lane >= dq // 2, q, zero)],
                                 axis=0)
        elif reps > 1:
            qs = jnp.concatenate([q[:, r * dq:(r + 1) * dq] for r in range(reps)], axis=0)
        else:
            qs = q

        def scores(c):
            return _dot_nt(k_ref[0, c * tk:(c + 1) * tk, hh * dq:(hh + 1) * dq], qs)

        m = l = acc = None
        st_next = scores(0)
        for c in range(nc):
            st, st_next = st_next, (scores(c + 1) if c + 1 < nc else None)
            cmax = jnp.max(st, axis=0, keepdims=True)
            m_new = cmax if c == 0 else jnp.maximum(m, cmax)
            p = jnp.exp2(st - m_new)
            psum = jnp.sum(p, axis=0, keepdims=True)
            pv = _dot(vt_ref[hh * dv:(hh + 1) * dv, c * tk:(c + 1) * tk], p.astype(BF16))
            if c == 0:
                l, acc = psum, pv
            else:
                alpha = jnp.exp2(m - m_new)
                l = alpha * l + psum
                acc = alpha * acc + pv
            m = m_new
        if diff:
            ot = acc[:, :tq] / l[:, :tq] - lam * (acc[:, tq:] / l[:, tq:])
            o_ref[0, :, hh * dv:(hh + 1) * dv] = (_rms(ot.T, subln_ref[...]) * (1.0 - lam_init)).astype(BF16)
        else:
            o = (acc / l).T
            for r in range(reps):
                o_ref[0, :, (hh * reps + r) * dv:(hh * reps + r + 1) * dv] = o[r * tq:(r + 1) * tq].astype(BF16)


def _attention(q, k, v, *, kv_heads, reps, dq, dv, diff=False, lam=None, subln=None, lam_init=0.0,
               q_off=0, k_off=0, v_off=0):
    b, s, _ = q.shape
    length = k.shape[1]
    stack = 2 if diff else reps
    tq = min(ATTN_ROWS // stack, s)
    tk = min(KV_CHUNK, length)
    heads = kv_heads if s == tq and length == tk else 1
    assert q_off % heads == 0 and k_off % heads == 0 and v_off % heads == 0
    in_specs = [pl.BlockSpec((1, tq, heads * reps * dq), lambda bi, h, i: (bi, i, q_off // heads + h)),
                pl.BlockSpec((1, length, heads * dq), lambda bi, h, i: (bi, 0, k_off // heads + h)),
                pl.BlockSpec((1, length, heads * dv), lambda bi, h, i: (bi, 0, v_off // heads + h))]
    args = [q, k, v]
    if diff:
        in_specs += [pl.BlockSpec(lam.shape, lambda bi, h, i: (0, 0)),
                     pl.BlockSpec(subln.shape, lambda bi, h, i: (0, 0))]
        args += [lam, subln]
    return pl.pallas_call(
        functools.partial(_attn_kernel, diff=diff, heads=heads, reps=reps, dq=dq, dv=dv, tq=tq, length=length,
                          tk=tk, lam_init=lam_init),
        out_shape=jax.ShapeDtypeStruct((b, s, kv_heads * reps * dv), BF16),
        grid=(b, kv_heads // heads, s // tq),
        in_specs=in_specs,
        out_specs=pl.BlockSpec((1, tq, heads * reps * dv), lambda bi, h, i: (bi, i, h)),
        scratch_shapes=[pltpu.VMEM((heads * dv, length), BF16)],
        compiler_params=_params("arbitrary", "arbitrary", "arbitrary"),
        name="diff_attention" if diff else "attention",
    )(*args)


def _out_proj_kernel(o_ref, w_ref, x_ref, mod_ref, g_ref, y_ref, *, gate_row):
    f = _dot(o_ref[...], w_ref[...])
    y_ref[...] = x_ref[...] + mod_ref[0, gate_row:gate_row + 1, :] * _rms(f, g_ref[...])


def _out_proj(o, w, x, mod, g, groups, gate_row):
    m, d = x.shape
    k = o.shape[1]
    t = m // groups
    tm = _tile_rows(t)
    nt = t // tm
    row = lambda gi, i: (gi * nt + i, 0)
    const = lambda gi, i: (0, 0)
    return pl.pallas_call(
        functools.partial(_out_proj_kernel, gate_row=gate_row),
        out_shape=jax.ShapeDtypeStruct((m, d), F32),
        grid=(groups, nt),
        in_specs=[pl.BlockSpec((tm, k), row),
                  pl.BlockSpec((k, d), const),
                  pl.BlockSpec((tm, d), row),
                  pl.BlockSpec((1, 6, d), lambda gi, i: (gi, 0, 0)),
                  pl.BlockSpec((1, d), const)],
        out_specs=pl.BlockSpec((tm, d), row),
        compiler_params=_params("arbitrary", "arbitrary"),
        name="out_proj_residual",
    )(o, w, x, mod, g)


HALO = 8


def _ffn_kernel(x_ref, xp_ref, xn_ref, mod_ref, g_ref, go_ref, wg_ref, wv_ref, cwg_ref, cwv_ref, cbg_ref,
                cbv_ref, wd_ref, y_ref, h_ref, act_ref, acc_ref, *, tm, tf, seq, nf):
    i = pl.program_id(1)
    j = pl.program_id(2)
    halo_only = seq % tm == 0

    @pl.when(j == 0)
    def _():
        shift, scale = mod_ref[0, 3:4, :], mod_ref[0, 4:5, :]
        h_ref[0:tm, :] = _modnorm(x_ref[...], g_ref[...], shift, scale).astype(BF16)
        halo = _modnorm(jnp.concatenate([xp_ref[...], xn_ref[...]], axis=0), g_ref[...], shift, scale)
        if halo_only:
            first = ((i * tm) % seq == 0).astype(F32)
            last = (((i + 1) * tm) % seq == 0).astype(F32)
            ones = jnp.ones((HALO, 1), F32)
            halo = halo * jnp.concatenate([ones - first, ones - last], axis=0)
        h_ref[tm:tm + 2 * HALO, :] = halo.astype(BF16)
        acc_ref[...] = jnp.zeros_like(acc_ref)

    if not halo_only:
        tok = i * tm + lax.broadcasted_iota(jnp.int32, (tm, 1), 0)
        seq_first = (tok % seq) == 0
        seq_last = (tok % seq) == seq - 1

    def conv(u, cw_ref, cb_ref, cols):
        ext = jnp.concatenate([u[tm:tm + HALO], u[0:tm], u[tm + HALO:tm + 2 * HALO]], axis=0)
        um = u[0:tm]
        up = pltpu.roll(ext, 1, 0)[HALO:HALO + tm]
        un = pltpu.roll(ext, tm + 2 * HALO - 1, 0)[HALO:HALO + tm]
        if not halo_only:
            up = jnp.where(seq_first, 0.0, up)
            un = jnp.where(seq_last, 0.0, un)
        return (up * cw_ref[0, 0:1, cols] + um * cw_ref[0, 1:2, cols] + un * cw_ref[0, 2:3, cols]
                + cb_ref[0, :, cols])

    def step(up, down, last, slot):
        def up_piece(c):
            cols = slice(c * MXU_COLS, (c + 1) * MXU_COLS)
            return _dot(h_ref[...], wg_ref[0, :, cols]), _dot(h_ref[...], wv_ref[0, :, cols])

        def gate_piece(c, ug, uv):
            cols = slice(c * MXU_COLS, (c + 1) * MXU_COLS)
            gate = conv(ug, cwg_ref, cbg_ref, cols)
            val = conv(uv, cwv_ref, cbv_ref, cols)
            act_ref[slot, :, cols] = (gate * (1.0 / (1.0 + jnp.exp(-gate))) * val).astype(BF16)

        def down_piece(c):
            cols = slice(c * MXU_COLS, (c + 1) * MXU_COLS)
            acc_ref[:, cols] += _dot(act_ref[1 - slot], wd_ref[0, :, cols])

        n_up = tf // MXU_COLS if up else 0
        n_down = acc_ref.shape[1] // MXU_COLS if down else 0
        pending = None
        for c in range(n_up):
            u = up_piece(c)
            if pending is not None:
                gate_piece(c - 1, *pending)
            pending = u
        for c in range(n_down):
            if c == n_down // 2 and pending is not None:
                gate_piece(n_up - 1, *pending)
                pending = None
            down_piece(c)
        if pending is not None:
            gate_piece(n_up - 1, *pending)
        if last:
            y_ref[...] = x_ref[...] + mod_ref[0, 5:6, :] * _rms(acc_ref[...], go_ref[...])

    for up, down, last in sorted({(s < nf, s >= 1, s == nf) for s in range(nf + 1)}):
        cond = ((j < nf) == up) & ((j >= 1) == down) & ((j == nf) == last)
        for parity in range(2):
            pl.when(cond & (j % 2 == parity))(functools.partial(step, up, down, last, parity))


FF_CHUNK = 512


def _ffn_chunks(ff):
    tf = FF_CHUNK if ff % FF_CHUNK == 0 else ff
    return tf, ff // tf


def _conv_ffn(x, mod, g_in, g_out, w_in, conv_w, conv_b, w_down, layer, groups, seq):
    m, d = x.shape
    ff = w_down.shape[1]
    t = m // groups
    tm = _tile_rows(t)
    nt = t // tm
    tf, nf = _ffn_chunks(ff)
    hb = tm // HALO
    last_halo = m // HALO - 1
    row = lambda gi, i, j: (gi * nt + i, 0)
    const = lambda gi, i, j: (0, 0)
    up = lambda j: jnp.minimum(j, nf - 1)
    down = lambda j: jnp.maximum(j - 1, 0)
    return pl.pallas_call(
        functools.partial(_ffn_kernel, tm=tm, tf=tf, seq=seq, nf=nf),
        out_shape=jax.ShapeDtypeStruct((m, d), F32),
        grid=(groups, nt, nf + 1),
        in_specs=[pl.BlockSpec((tm, d), row),
                  pl.BlockSpec((HALO, d), lambda gi, i, j: (jnp.maximum((gi * nt + i) * hb - 1, 0), 0)),
                  pl.BlockSpec((HALO, d), lambda gi, i, j: (jnp.minimum((gi * nt + i + 1) * hb, last_halo), 0)),
                  pl.BlockSpec((1, 6, d), lambda gi, i, j: (gi, 0, 0)),
                  pl.BlockSpec((1, d), const),
                  pl.BlockSpec((1, d), const),
                  pl.BlockSpec((1, d, tf), lambda gi, i, j: (layer, 0, up(j))),
                  pl.BlockSpec((1, d, tf), lambda gi, i, j: (layer, 0, nf + up(j))),
                  pl.BlockSpec((1, 3, tf), lambda gi, i, j: (layer, 0, up(j))),
                  pl.BlockSpec((1, 3, tf), lambda gi, i, j: (layer, 0, nf + up(j))),
                  pl.BlockSpec((1, 1, tf), lambda gi, i, j: (layer, 0, up(j))),
                  pl.BlockSpec((1, 1, tf), lambda gi, i, j: (layer, 0, nf + up(j))),
                  pl.BlockSpec((1, tf, d), lambda gi, i, j: (layer, down(j), 0))],
        out_specs=pl.BlockSpec((tm, d), row),
        scratch_shapes=[pltpu.VMEM((tm + 2 * HALO, d), BF16), pltpu.VMEM((2, tm, tf), BF16),
                        pltpu.VMEM((tm, d), F32)],
        compiler_params=_params("arbitrary", "arbitrary", "arbitrary"),
        name="conv_ffn",
    )(x, x, x, mod, g_in, g_out, w_in, w_in, conv_w, conv_w, conv_b, conv_b, w_down)


def _pad_lanes(a, width):
    return jnp.pad(a, [(0, 0)] * (a.ndim - 1) + [(0, width - a.shape[-1])])


def _mla_mixer(x, mod, g, groups, t, sample, seq, cache_ckv, cache_kpe, w_down, q_norm, kv_norm, w_uq, w_ukv):
    m = x.shape[0]
    ql, kvl = q_norm.shape[0], kv_norm.shape[0]
    scale = LOG2E / math.sqrt(NOPE_A + ROPE_A)
    kpe_tables = _rope_tables(t, ROPE_A, 0, LANES - ROPE_A) if sample else None
    q_tables = kpe_tables
    w_down_p = _pad_lanes(w_down, ql + kvl + LANES).astype(BF16)
    cq, ckv, kpe = _mla_down(x, mod, g, w_down_p, q_norm[None], kv_norm[None], kpe_tables, groups)
    w_uq_p = _pad_lanes(w_uq.reshape(ql, H_A, NOPE_A + ROPE_A), 2 * LANES).reshape(ql, H_A * 2 * LANES)
    q = _mla_uq(cq, w_uq_p.astype(BF16), q_tables, t, scale)
    if sample:
        b = groups
        ckv_all = jnp.concatenate([cache_ckv.astype(BF16), ckv.astype(BF16).reshape(b, t, kvl)], axis=1)
        kpe_all = jnp.concatenate([_pad_lanes(cache_kpe, LANES).astype(BF16),
                                   kpe.astype(BF16).reshape(b, t, LANES)], axis=1)
        length = ckv_all.shape[1]
        s = t
    else:
        s = seq
        b = m // s
        ckv_all, kpe_all, length = ckv.astype(BF16), kpe.astype(BF16), s
    k, v = _mla_ukv(ckv_all.reshape(b * length, kvl), w_ukv.astype(BF16), kpe_all.reshape(b * length, LANES))
    o = _attention(q.reshape(b, s, H_A * 2 * LANES), k.reshape(b, length, H_A * 2 * LANES),
                   v.reshape(b, length, H_A * V_A), kv_heads=H_A, reps=1, dq=2 * LANES, dv=V_A)
    return o, ckv, kpe


def _diff_mixer(x, mod, g, groups, t, sample, seq, cache_k, cache_v, w_qkv, lam_vecs, subln, lam_init):
    m = x.shape[0]
    d_b = H_B * 2 * DH_B
    tables = _rope_tables(t, DH_B, 0, 0, reps=2) if sample else None
    outs = _qkv_proj(x, mod, g, w_qkv.astype(BF16), groups, tn=min(1024, d_b), n_q=d_b // min(1024, d_b),
                     n_k=d_b // min(1024, d_b), pattern=2 * DH_B, half=DH_B // 4,
                     q_scale=LOG2E / math.sqrt(DH_B), tables=tables, want_f32=not sample)
    qkv = outs[0]
    hb = d_b // (2 * DH_B)
    if sample:
        b = groups
        qkv = qkv.reshape(b, t, 3 * d_b)
        past = cache_k.shape[1]
        k = jnp.concatenate([cache_k.reshape(b, past, d_b).astype(BF16), qkv[:, :, d_b:2 * d_b]], axis=1)
        v = jnp.concatenate([cache_v.reshape(b, past, d_b).astype(BF16), qkv[:, :, 2 * d_b:]], axis=1)
        o = _attention(qkv, k, v, kv_heads=H_B, reps=1, dq=2 * DH_B, dv=2 * DH_B, diff=True, lam=lam_vecs,
                       subln=subln[None], lam_init=lam_init)
        return o, None
    b = m // seq
    qkv = qkv.reshape(b, seq, 3 * d_b)
    o = _attention(qkv, qkv, qkv, kv_heads=H_B, reps=1, dq=2 * DH_B, dv=2 * DH_B, diff=True, lam=lam_vecs,
                   subln=subln[None], lam_init=lam_init, k_off=hb, v_off=2 * hb)
    return o, outs[1]


def _gqa_mixer(x, mod, g, groups, t, sample, seq, cache_k, cache_v, w_qkv, q_norm, k_norm):
    m = x.shape[0]
    dq_all, dk_all = H_C * HD_C, KVH_C * HD_C
    tn = dk_all
    tables = _rope_tables(t, HD_C, 0, 0) if sample else None
    outs = _qkv_proj(x, mod, g, w_qkv.astype(BF16), groups, tn=tn, n_q=dq_all // tn, n_k=1, pattern=HD_C,
                     half=HD_C // 4, q_scale=LOG2E / math.sqrt(HD_C), tables=tables,
                     norms=(q_norm[None], k_norm[None]), want_f32=not sample)
    qkv = outs[0]
    reps = H_C // KVH_C
    if sample:
        b = groups
        qkv = qkv.reshape(b, t, dq_all + 2 * dk_all)
        past = cache_k.shape[1]
        k = jnp.concatenate([cache_k.reshape(b, past, dk_all).astype(BF16),
                             qkv[:, :, dq_all:dq_all + dk_all]], axis=1)
        v = jnp.concatenate([cache_v.reshape(b, past, dk_all).astype(BF16), qkv[:, :, dq_all + dk_all:]], axis=1)
        o = _attention(qkv, k, v, kv_heads=KVH_C, reps=reps, dq=HD_C, dv=HD_C)
        return o, None
    b = m // seq
    qkv = qkv.reshape(b, seq, dq_all + 2 * dk_all)
    o = _attention(qkv, qkv, qkv, kv_heads=KVH_C, reps=reps, dq=HD_C, dv=HD_C, k_off=H_C, v_off=H_C + KVH_C)
    return o, outs[1]


def _lambda_init(layer):
    return 0.8 - 0.6 * math.exp(-0.3 * layer)


def kernel(x_prompt, x_sample, c, cache_a_ckv, cache_a_kpe, cache_b_k, cache_b_v, cache_c_k, cache_c_v, c_ctx, norm_g, w_mod, b_mod, ffn_w_in, ffn_conv_w, ffn_conv_b, ffn_w_down, a_w_down, a_q_norm, a_kv_norm, a_w_uq, a_w_ukv, a_w_o, b_w_qkv, b_lambda, b_subln, b_w_o, c_w_qkv, c_q_norm, c_k_norm, c_w_o):
    batch, seq, d = x_prompt.shape
    dec_batch, dec_seq, _ = x_sample.shape
    depth = w_mod.shape[0]
    tp = batch * seq

    cvec = jnp.concatenate([c, c_ctx[None], jnp.zeros((8 - dec_batch - 1, d), F32)], axis=0)
    mods = _modulation(cvec, w_mod, b_mod)

    xp = x_prompt.reshape(tp, d)
    xs = x_sample.reshape(dec_batch * dec_seq, d)
    kv_lora = a_kv_norm.shape[1]
    st = [[] for _ in range(6)]
    w_in_all = ffn_w_in.astype(BF16)
    w_down_all = ffn_w_down.astype(BF16)
    conv_b_all = ffn_conv_b[:, None, :]

    for l in range(depth):
        kind, j = l % N_MIXERS, l // N_MIXERS
        mod_s = mods[l, :dec_batch].reshape(dec_batch, 6, d)
        mod_p = mods[l, dec_batch:dec_batch + 1].reshape(1, 6, d)
        g = norm_g[l]
        streams = ((xp, mod_p, 1, tp, False, seq), (xs, mod_s, dec_batch, dec_seq, True, dec_seq))
        new = []
        for x, mod, groups, t, sample, sq in streams:
            if kind == 0:
                o, ckv, kpe = _mla_mixer(x, mod, g[0:1], groups, t, sample, sq, cache_a_ckv[:, j], cache_a_kpe[:, j],
                                         a_w_down[j], a_q_norm[j], a_kv_norm[j], a_w_uq[j], a_w_ukv[j])
                if not sample:
                    st[0].append(ckv.reshape(batch, seq, kv_lora))
                    st[1].append(kpe[:, :ROPE_A].reshape(batch, seq, ROPE_A))
                w_o = a_w_o[j]
            elif kind == 1:
                o, full = _diff_mixer(x, mod, g[0:1], groups, t, sample, sq, cache_b_k[:, j], cache_b_v[:, j],
                                      b_w_qkv[j], b_lambda[j], b_subln[j], _lambda_init(l))
                if not sample:
                    d_b = H_B * 2 * DH_B
                    st[2].append(full[:, d_b:2 * d_b].reshape(batch, seq, H_B, 2, DH_B))
                    st[3].append(full[:, 2 * d_b:].reshape(batch, seq, H_B, 2 * DH_B))
                w_o = b_w_o[j]
            else:
                o, full = _gqa_mixer(x, mod, g[0:1], groups, t, sample, sq, cache_c_k[:, j], cache_c_v[:, j],
                                     c_w_qkv[j], c_q_norm[j], c_k_norm[j])
                if not sample:
                    dq_all, dk_all = H_C * HD_C, KVH_C * HD_C
                    st[4].append(full[:, dq_all:dq_all + dk_all].reshape(batch, seq, KVH_C, HD_C))
                    st[5].append(full[:, dq_all + dk_all:].reshape(batch, seq, KVH_C, HD_C))
                w_o = c_w_o[j]
            x = _out_proj(o.reshape(x.shape[0], -1), w_o.astype(BF16), x, mod, g[1:2], groups, 2)
            x = _conv_ffn(x, mod, g[2:3], g[3:4], w_in_all, ffn_conv_w, conv_b_all, w_down_all, l, groups, sq)
            new.append(x)
        xp, xs = new

    states = tuple(jnp.stack(s, axis=1) for s in st)
    return (xp.reshape(batch, seq, d), xs.reshape(dec_batch, dec_seq, d)) + states
```
